```python
import math
import jax, jax.numpy as jnp
from jax import lax
import numpy as np

D_MODEL = 1024
BATCH = 2
SEQ = 8192
DEPTH = 2
DEC_BATCH = 128
DEC_SEQ = 8
PAST_LEN = 16384
PAGE_SIZE = 128

F32 = jnp.float32
EPS = 1e-6
N_BRANCH = 3
LRU_WIDTH = 512
LRU_BLOCKS = 8
LRU_BLOCK = LRU_WIDTH // LRU_BLOCKS
CONV_W = 4
LRU_C = 8.0
RW_HEADS = 8
RW_HD = 64
RW_WIDTH = RW_HEADS * RW_HD
DECAY_LORA = 64
AAA_LORA = 64
GN_EPS = 64e-5
SHIFT_COLS = 3 * RW_WIDTH + DECAY_LORA + AAA_LORA
SHIFT_SPLITS = (RW_WIDTH, 2 * RW_WIDTH, 3 * RW_WIDTH, 3 * RW_WIDTH + DECAY_LORA)
MLA_HEADS = 8
Q_RANK = 256
KV_RANK = 128
NOPE_DIM = 64
ROPE_DIM = 32
V_DIM = 64
QK_DIM = NOPE_DIM + ROPE_DIM
MLA_WIDTH = MLA_HEADS * V_DIM
ROPE_THETA = 10000.0
SCALE = QK_DIM ** -0.5
Q_BLOCK = 128
IN_SIZES = (LRU_WIDTH, LRU_WIDTH,
            SHIFT_COLS, RW_WIDTH,
            Q_RANK, KV_RANK, ROPE_DIM, MLA_WIDTH,
            N_BRANCH * D_MODEL)
IN_COLS = sum(IN_SIZES)
IN_SPLITS = tuple(int(s) for s in np.cumsum(IN_SIZES)[:-1])

kernel_name = 'hybrid_rglru_rwkv7_mla_decoder_step'


def rms_norm(x, g):
    xf = x.astype(F32)
    y = xf * lax.rsqrt(jnp.mean(xf * xf, axis=-1, keepdims=True) + EPS)
    return (y * g).astype(x.dtype)


def rope_tables(pos):
    inv = 1.0 / (ROPE_THETA ** (jnp.arange(0, ROPE_DIM, 2, dtype=F32) / ROPE_DIM))
    ang = pos.astype(F32)[:, None] * inv[None, :]
    return jnp.cos(ang), jnp.sin(ang)


def apply_rope(x, cos, sin):
    x1, x2 = jnp.split(x, 2, axis=-1)
    c = cos[:, None, :]
    s = sin[:, None, :]
    return jnp.concatenate([x1 * c - x2 * s, x1 * s + x2 * c], axis=-1).astype(x.dtype)


def causal_conv(u, buf, w, b):
    T = u.shape[1]
    full = jnp.concatenate([buf.astype(u.dtype), u], axis=1)
    out = b + sum(full[:, k:k + T] * w[k] for k in range(CONV_W))
    return out.astype(u.dtype), full[:, T:]


def rg_lru(xc, h0, w_r, b_r, w_i, b_i, lam):
    B, T, _ = xc.shape
    xb = xc.reshape(B, T, LRU_BLOCKS, LRU_BLOCK)
    r = jax.nn.sigmoid(jnp.einsum('btnc,ncd->btnd', xb, w_r).reshape(B, T, LRU_WIDTH) + b_r)
    i = jax.nn.sigmoid(jnp.einsum('btnc,ncd->btnd', xb, w_i).reshape(B, T, LRU_WIDTH) + b_i)
    log_a = -LRU_C * r.astype(F32) * jax.nn.softplus(-lam.astype(F32))
    a = jnp.exp(log_a)
    u = jnp.sqrt(-jnp.expm1(2.0 * log_a)) * (i * xc).astype(F32)

    def combine(e1, e2):
        a1, b1 = e1
        a2, b2 = e2
        return a1 * a2, a2 * b1 + b2

    a_cum, b_cum = lax.associative_scan(combine, (a, u), axis=1)
    h = a_cum * h0.astype(F32)[:, None, :] + b_cum
    return h.astype(xc.dtype), h[:, -1].astype(h0.dtype)


def rwkv7_mix(s, prev, S0, p):
    B, T, _ = s.shape
    dt = s.dtype
    s_prev = jnp.concatenate([prev[:, None, :].astype(dt), s[:, :-1]], axis=1)
    xs = s + (s_prev - s) * p['rw_mu']
    r, k, v, wd, ad = jnp.split(xs, SHIFT_SPLITS, axis=-1)
    w = -jax.nn.softplus(-(p['rw_w0'] + jnp.tanh(wd) @ p['rw_w2']).astype(F32)) - 0.5
    decay = jnp.exp(-jnp.exp(w))
    a = jax.nn.sigmoid((p['rw_a0'] + ad @ p['rw_a2']).astype(F32))

    def heads(t):
        return t.reshape(B, T, RW_HEADS, RW_HD)

    kf = heads(k.astype(F32))
    kk = kf * p['rw_kk'].reshape(RW_HEADS, RW_HD)
    kk = kk * lax.rsqrt(jnp.sum(kk * kk, axis=-1, keepdims=True) + 1e-12)
    a_h = heads(a)
    kf = kf * (1.0 + (a_h - 1.0) * p['rw_ka'].reshape(RW_HEADS, RW_HD))
    rf = heads(r.astype(F32))
    vf = heads(v.astype(F32))

    def step(S, inp):
        r_t, w_t, k_t, v_t, kk_t, b_t = inp
        sa = jnp.einsum('bhij,bhj->bhi', S, kk_t)
        S = (S * w_t[:, :, None, :] - sa[..., None] * b_t[:, :, None, :]
             + v_t[..., None] * k_t[:, :, None, :])
        return S, jnp.einsum('bhij,bhj->bhi', S, r_t)

    tm = lambda t: jnp.moveaxis(t, 1, 0)
    S_T, o = lax.scan(step, S0.astype(F32),
                      (tm(rf), tm(heads(decay)), tm(kf), tm(vf), tm(kk), tm(kk * a_h)))
    o = jnp.moveaxis(o, 0, 1)
    mean = jnp.mean(o, axis=-1, keepdims=True)
    var = jnp.mean(jnp.square(o - mean), axis=-1, keepdims=True)
    o = ((o - mean) * lax.rsqrt(var + GN_EPS)).reshape(B, T, RW_WIDTH) * p['rw_gn_g'] + p['rw_gn_b']
    bonus = jnp.sum(rf * kf * p['rw_rk'], axis=-1, keepdims=True) * vf
    o = o + bonus.reshape(B, T, RW_WIDTH)
    return o.astype(dt), S_T.astype(S0.dtype)


def mla_queries(c_q, pos, g_qa, w_uq, g_qn):
    q = jnp.einsum('btr,rhd->bthd', rms_norm(c_q, g_qa), w_uq)
    q = rms_norm(q, g_qn)
    cos, sin = rope_tables(pos)
    return jnp.concatenate([q[..., :NOPE_DIM], apply_rope(q[..., NOPE_DIM:], cos, sin)], axis=-1)


def mla_keys(ckv, kpe, cos, sin, w_ukv, g_kn):
    B, T, _ = ckv.shape
    kv = jnp.einsum('btr,rhd->bthd', ckv, w_ukv)
    k = jnp.concatenate([kv[..., :NOPE_DIM],
                         jnp.broadcast_to(kpe[:, :, None, :], (B, T, MLA_HEADS, ROPE_DIM)).astype(kv.dtype)], axis=-1)
    k = rms_norm(k, g_kn)
    k = jnp.concatenate([k[..., :NOPE_DIM], apply_rope(k[..., NOPE_DIM:], cos, sin)], axis=-1)
    return k, kv[..., NOPE_DIM:]


def mla_prompt_attn(q, k, v):
    B, T, H, _ = q.shape
    nb = T // Q_BLOCK
    qb = jnp.moveaxis(q.reshape(B, nb, Q_BLOCK, H, QK_DIM), 1, 0)
    kpos = jnp.arange(T)

    def one_block(args):
        qi, bi = args
        s = jnp.einsum('bqhd,bkhd->bhqk', qi, k).astype(F32) * SCALE
        qpos = bi * Q_BLOCK + jnp.arange(Q_BLOCK)
        s = jnp.where(kpos[None, :] <= qpos[:, None], s, -jnp.inf)
        pr = jax.nn.softmax(s, axis=-1).astype(v.dtype)
        return jnp.einsum('bhqk,bkhd->bqhd', pr, v)

    o = lax.map(one_block, (qb, jnp.arange(nb)))
    return jnp.moveaxis(o, 0, 1).reshape(B, T, H * V_DIM).astype(q.dtype)


def mla_sample_attn(q, ckv_new, kpe_new, pos, pool_ckv, pool_kpe, layer, page_table, w_ukv, g_kn):
    DB, Ts, H, _ = q.shape
    n_pages = page_table.shape[1]

    def update(carry, s, v):
        m, l, acc = carry
        m_new = jnp.maximum(m, jnp.max(s, axis=-1))
        corr = jnp.exp(m - m_new)
        pe = jnp.exp(s - m_new[..., None])
        return (m_new, l * corr + jnp.sum(pe, axis=-1),
                acc * corr[..., None] + jnp.einsum('bhqk,bkhd->bhqd', pe, v.astype(F32)))

    def page_step(carry, pg):
        phys = page_table[:, pg]
        cos, sin = rope_tables(pg * PAGE_SIZE + jnp.arange(PAGE_SIZE))
        k, v = mla_keys(pool_ckv[layer, phys].astype(q.dtype), pool_kpe[layer, phys].astype(q.dtype),
                        cos, sin, w_ukv, g_kn)
        s = jnp.einsum('bqhd,bkhd->bhqk', q, k).astype(F32) * SCALE
        return update(carry, s, v), None

    init = (jnp.full((DB, H, Ts), -jnp.inf, F32), jnp.zeros((DB, H, Ts), F32),
            jnp.zeros((DB, H, Ts, V_DIM), F32))
    carry, _ = lax.scan(page_step, init, jnp.arange(n_pages))
    cos, sin = rope_tables(pos)
    k, v = mla_keys(ckv_new, kpe_new, cos, sin, w_ukv, g_kn)
    s = jnp.einsum('bqhd,bkhd->bhqk', q, k).astype(F32) * SCALE
    causal = jnp.arange(Ts)[None, :] <= jnp.arange(Ts)[:, None]
    s = jnp.where(causal, s, -jnp.inf)
    m, l, acc = update(carry, s, v)
    o = acc / l[..., None]
    return jnp.transpose(o, (0, 2, 1, 3)).reshape(DB, Ts, H * V_DIM).astype(q.dtype)


def mixer_layer(x, pos, lru_h0, conv_buf, rw_S0, rw_prev, p, attend):
    h = rms_norm(x, p['norm_g'])
    z = h @ p['w_in']
    a_x, a_g, b_s, b_g, c_q, c_kv, c_pe, c_g, gate_logits = jnp.split(z, IN_SPLITS, axis=-1)
    a_c, conv_new = causal_conv(a_x, conv_buf, p['conv_w'], p['conv_b'])
    a_y, lru_h = rg_lru(a_c, lru_h0, p['lru_wr'], p['lru_br'], p['lru_wi'], p['lru_bi'], p['lru_lam'])
    o_a = a_y * jax.nn.silu(a_g)
    b_y, rw_S = rwkv7_mix(b_s, rw_prev, rw_S0, p)
    o_b = b_y * jax.nn.silu(b_g)
    ckv = rms_norm(c_kv, p['mla_g_kva'])
    q = mla_queries(c_q, pos, p['mla_g_qa'], p['mla_w_uq'], p['mla_g_qn'])
    o_c = attend(q, ckv, c_pe) * jax.nn.silu(c_g)
    g_a, g_b, g_c = jnp.split(jax.nn.sigmoid(gate_logits), N_BRANCH, axis=-1)
    merged = g_a * (o_a @ p['w_oa']) + g_b * (o_b @ p['w_ob']) + g_c * (o_c @ p['w_oc'])
    y = x + merged @ p['w_out']
    return y, (lru_h, conv_new, rw_S, b_s[:, -1], ckv, c_pe)


def setup_inputs(seed: int = 0) -> dict:
    key = jax.random.key(seed)
    keys = iter(jax.random.split(key, 64))

    def nrm(shape, scale):
        return jax.random.normal(next(keys), shape, F32) * scale

    def gain(shape):
        return 1.0 + nrm(shape, 0.02)

    n_pages = PAST_LEN // PAGE_SIZE
    n_used = DEC_BATCH * n_pages
    n_pool = n_used + n_used // 4
    page_table = jax.random.permutation(next(keys), n_pool)[:n_used].reshape(DEC_BATCH, n_pages).astype(jnp.int32)
    a_pow = jax.random.uniform(next(keys), (DEPTH, LRU_WIDTH), F32, 0.9, 0.999)
    a_base = a_pow ** (1.0 / LRU_C)
    lru_lam = jnp.log(a_base) - jnp.log1p(-a_base)
    return {
        'x_prompt': nrm((BATCH, SEQ, D_MODEL), 1.0),
        'x_sample': nrm((DEC_BATCH, DEC_SEQ, D_MODEL), 1.0),
        'state_lru_h': nrm((DEPTH, DEC_BATCH, LRU_WIDTH), 0.5),
        'state_lru_conv': nrm((DEPTH, DEC_BATCH, CONV_W - 1, LRU_WIDTH), 1.0),
        'state_rwkv_S': nrm((DEPTH, DEC_BATCH, RW_HEADS, RW_HD, RW_HD), 0.3),
        'state_rwkv_shift': nrm((DEPTH, DEC_BATCH, SHIFT_COLS), 1.0),
        'cache_ckv': nrm((DEPTH, n_pool, PAGE_SIZE, KV_RANK), 1.0),
        'cache_kpe': nrm((DEPTH, n_pool, PAGE_SIZE, ROPE_DIM), 1.0),
        'page_table': page_table,
        'norm_g': gain((DEPTH, D_MODEL)),
        'w_in': nrm((DEPTH, D_MODEL, IN_COLS), D_MODEL ** -0.5),
        'conv_w': nrm((DEPTH, CONV_W, LRU_WIDTH), CONV_W ** -0.5),
        'conv_b': nrm((DEPTH, LRU_WIDTH), 0.02),
        'lru_wr': nrm((DEPTH, LRU_BLOCKS, LRU_BLOCK, LRU_BLOCK), LRU_BLOCK ** -0.5),
        'lru_br': nrm((DEPTH, LRU_WIDTH), 0.02),
        'lru_wi': nrm((DEPTH, LRU_BLOCKS, LRU_BLOCK, LRU_BLOCK), LRU_BLOCK ** -0.5),
        'lru_bi': nrm((DEPTH, LRU_WIDTH), 0.02),
        'lru_lam': lru_lam,
        'rw_mu': jax.random.uniform(next(keys), (DEPTH, SHIFT_COLS), F32),
        'rw_w0': jax.random.uniform(next(keys), (DEPTH, RW_WIDTH), F32, -6.0, -1.0),
        'rw_w2': nrm((DEPTH, DECAY_LORA, RW_WIDTH), 0.1 * DECAY_LORA ** -0.5),
        'rw_a0': nrm((DEPTH, RW_WIDTH), 0.1),
        'rw_a2': nrm((DEPTH, AAA_LORA, RW_WIDTH), AAA_LORA ** -0.5),
        'rw_kk': 0.85 + nrm((DEPTH, RW_WIDTH), 0.02),
        'rw_ka': gain((DEPTH, RW_WIDTH)),
        'rw_rk': nrm((DEPTH, RW_HEADS, RW_HD), 0.1),
        'rw_gn_g': gain((DEPTH, RW_WIDTH)),
        'rw_gn_b': nrm((DEPTH, RW_WIDTH), 0.02),
        'mla_g_qa': gain((DEPTH, Q_RANK)),
        'mla_g_kva': gain((DEPTH, KV_RANK)),
        'mla_w_uq': nrm((DEPTH, Q_RANK, MLA_HEADS, QK_DIM), Q_RANK ** -0.5),
        'mla_w_ukv': nrm((DEPTH, KV_RANK, MLA_HEADS, NOPE_DIM + V_DIM), KV_RANK ** -0.5),
        'mla_g_qn': gain((DEPTH, QK_DIM)),
        'mla_g_kn': gain((DEPTH, QK_DIM)),
        'w_oa': nrm((DEPTH, LRU_WIDTH, D_MODEL), LRU_WIDTH ** -0.5),
        'w_ob': nrm((DEPTH, RW_WIDTH, D_MODEL), RW_WIDTH ** -0.5),
        'w_oc': nrm((DEPTH, MLA_WIDTH, D_MODEL), MLA_WIDTH ** -0.5),
        'w_out': nrm((DEPTH, D_MODEL, D_MODEL), D_MODEL ** -0.5),
    }


def reference(x_prompt, x_sample, state_lru_h, state_lru_conv, state_rwkv_S, state_rwkv_shift,
              cache_ckv, cache_kpe, page_table, norm_g, w_in, conv_w, conv_b, lru_wr, lru_br,
              lru_wi, lru_bi, lru_lam, rw_mu, rw_w0, rw_w2, rw_a0, rw_a2, rw_kk, rw_ka, rw_rk,
              rw_gn_g, rw_gn_b, mla_g_qa, mla_g_kva, mla_w_uq, mla_w_ukv, mla_g_qn, mla_g_kn,
              w_oa, w_ob, w_oc, w_out):
    B, T, _ = x_prompt.shape
    DB, Ts, _ = x_sample.shape
    dt = x_prompt.dtype
    pos_p = jnp.arange(T)
    pos_s = PAST_LEN + jnp.arange(Ts)
    cos_p, sin_p = rope_tables(pos_p)
    hp = x_prompt
    hs = x_sample
    st_p_all = []
    st_s_all = []
    for l in range(DEPTH):
        p = {
            'norm_g': norm_g[l], 'w_in': w_in[l], 'conv_w': conv_w[l], 'conv_b': conv_b[l],
            'lru_wr': lru_wr[l], 'lru_br': lru_br[l], 'lru_wi': lru_wi[l], 'lru_bi': lru_bi[l],
            'lru_lam': lru_lam[l], 'rw_mu': rw_mu[l], 'rw_w0': rw_w0[l], 'rw_w2': rw_w2[l],
            'rw_a0': rw_a0[l], 'rw_a2': rw_a2[l], 'rw_kk': rw_kk[l], 'rw_ka': rw_ka[l],
            'rw_rk': rw_rk[l], 'rw_gn_g': rw_gn_g[l], 'rw_gn_b': rw_gn_b[l],
            'mla_g_qa': mla_g_qa[l], 'mla_g_kva': mla_g_kva[l], 'mla_w_uq': mla_w_uq[l],
            'mla_w_ukv': mla_w_ukv[l], 'mla_g_qn': mla_g_qn[l], 'mla_g_kn': mla_g_kn[l],
            'w_oa': w_oa[l], 'w_ob': w_ob[l], 'w_oc': w_oc[l], 'w_out': w_out[l],
        }

        def attend_prompt(q, ckv, kpe):
            k, v = mla_keys(ckv, kpe, cos_p, sin_p, p['mla_w_ukv'], p['mla_g_kn'])
            return mla_prompt_attn(q, k, v)

        def attend_sample(q, ckv, kpe):
            return mla_sample_attn(q, ckv, kpe, pos_s, cache_ckv, cache_kpe, l, page_table,
                                   p['mla_w_ukv'], p['mla_g_kn'])

        hp, st_p = mixer_layer(hp, pos_p,
                               jnp.zeros((B, LRU_WIDTH), dt),
                               jnp.zeros((B, CONV_W - 1, LRU_WIDTH), dt),
                               jnp.zeros((B, RW_HEADS, RW_HD, RW_HD), dt),
                               jnp.zeros((B, SHIFT_COLS), dt),
                               p, attend_prompt)
        hs, st_s = mixer_layer(hs, pos_s, state_lru_h[l], state_lru_conv[l], state_rwkv_S[l],
                               state_rwkv_shift[l], p, attend_sample)
        st_p_all.append(st_p)
        st_s_all.append(st_s)

    def stack(states, i):
        return jnp.stack([s[i] for s in states], axis=0)

    p_lru_h, p_lru_conv, p_rwkv_S = stack(st_p_all, 0), stack(st_p_all, 1), stack(st_p_all, 2)
    p_rwkv_shift, p_ckv, p_kpe = stack(st_p_all, 3), stack(st_p_all, 4), stack(st_p_all, 5)
    s_lru_h, s_lru_conv, s_rwkv_S = stack(st_s_all, 0), stack(st_s_all, 1), stack(st_s_all, 2)
    s_rwkv_shift, s_ckv, s_kpe = stack(st_s_all, 3), stack(st_s_all, 4), stack(st_s_all, 5)
    return (hp, hs, p_lru_h, p_lru_conv, p_rwkv_S, p_rwkv_shift, p_ckv, p_kpe,
            s_lru_h, s_lru_conv, s_rwkv_S, s_rwkv_shift, s_ckv, s_kpe)
```

```python
import functools
import math

import jax
import jax.numpy as jnp
from jax import lax
from jax.experimental import pallas as pl
from jax.experimental.pallas import tpu as pltpu

F32 = jnp.float32
BF16 = jnp.bfloat16

EPS = 1e-6
LRU_C = 8.0
GN_EPS = 64e-5
ROPE_THETA = 10000.0
NEG = -1e30

LANES = 128
SUBLANES = 8
VMEM_LIMIT_BYTES = 56 * 1024 * 1024

LRU_WIDTH = 512
RW_WIDTH = 512
RW_HD = 64
RW_PAIRS = RW_WIDTH // LANES
LORA = 64
SHIFT_COLS = 3 * RW_WIDTH + 2 * LORA
Q_RANK = 256
KV_RANK = 128
MLA_HEADS = 8
NOPE_DIM = 64
ROPE_DIM = 32
V_DIM = 64
QK_DIM = NOPE_DIM + ROPE_DIM
MLA_WIDTH = MLA_HEADS * V_DIM
SCALE = QK_DIM ** -0.5
RW_BLOCK_ROWS = 64

SEG_AX = (0, 512)
SEG_AG = (512, 1024)
SEG_BS = (1024, 2688)
SEG_BG = (2688, 3200)
SEG_CQ = (3200, 3456)
SEG_CKV = (3456, 3584)
SEG_PEA = (3584, 3712)
SEG_PEB = (3712, 3840)
SEG_CG = (3840, 4352)
SEG_GATE = (4352, 7424)
IN_SEGS = (SEG_AX, SEG_AG, SEG_BS, SEG_BG, SEG_CQ, SEG_CKV, SEG_PEA, SEG_PEB, SEG_CG, SEG_GATE)
IN_COLS_PAD = SEG_GATE[1]


def _cparams(sem):
    return pltpu.CompilerParams(dimension_semantics=sem, vmem_limit_bytes=VMEM_LIMIT_BYTES)


def _const_spec(shape):
    nd = len(shape)
    return pl.BlockSpec(shape, lambda *_: (0,) * nd)


def _sigmoid(x):
    return jax.nn.sigmoid(x)


def _softplus(x):
    return jnp.maximum(x, 0.0) + jnp.log(1.0 + jnp.exp(-jnp.abs(x)))


def _mod_pow2(x, n):
    return jnp.bitwise_and(x, n - 1)


def _div_pow2(x, n):
    return lax.shift_right_logical(x, int(math.log2(n)))


def _dot(a, b):
    return jnp.dot(a, b, preferred_element_type=F32)


def _dot_nt(a, b):
    return lax.dot_general(a, b, (((1,), (1,)), ((), ())), preferred_element_type=F32)


def _dot_tn(a, b):
    return lax.dot_general(a, b, (((0,), (0,)), ((), ())), preferred_element_type=F32)


def _split2(x):
    hi = x.astype(BF16)
    lo = (x - hi.astype(F32)).astype(BF16)
    return hi, lo


def _split3(x):
    hi = x.astype(BF16)
    r1 = x - hi.astype(F32)
    mid = r1.astype(BF16)
    lo = (r1 - mid.astype(F32)).astype(BF16)
    return hi, mid, lo


def _dot_sel_r(x, m01):
    hi, lo = _split2(x)
    return _dot(hi, m01) + _dot(lo, m01)


def _dot_sel_l3(m01, x):
    hi, mid, lo = _split3(x)
    return _dot(m01, hi) + _dot(m01, mid) + _dot(m01, lo)


def _dot_x3(a, b):
    ah, al = _split2(a)
    bh, bl = _split2(b)
    return _dot(ah, bh) + _dot(ah, bl) + _dot(al, bh)


def _inproj_kernel(x_ref, g_ref, w_ref, *out_refs):
    x = x_ref[...]
    ms = jnp.mean(x * x, axis=-1, keepdims=True)
    hn = (x * lax.rsqrt(ms + EPS) * g_ref[...]).astype(BF16)
    for o_ref, (lo, hi) in zip(out_refs, IN_SEGS):
        for c0 in range(lo, hi, 512):
            c1 = min(c0 + 512, hi)
            o_ref[:, c0 - lo:c1 - lo] = _dot(hn, w_ref[:, c0:c1])


def _inproj(x2d, g, w_pad):
    n, d = x2d.shape
    tm = min(256, n)
    out_shape = [jax.ShapeDtypeStruct((n, hi - lo), F32) for lo, hi in IN_SEGS]
    out_specs = [pl.BlockSpec((tm, hi - lo), lambda i: (i, 0)) for lo, hi in IN_SEGS]
    return pl.pallas_call(
        _inproj_kernel,
        grid=(n // tm,),
        in_specs=[
            pl.BlockSpec((tm, d), lambda i: (i, 0)),
            _const_spec((1, d)),
            pl.BlockSpec((d, IN_COLS_PAD), lambda i: (0, 0), pipeline_mode=pl.Buffered(1)),
        ],
        out_specs=out_specs,
        out_shape=out_shape,
        compiler_params=_cparams(("arbitrary",)),
        name="inproj",
    )(x2d, g, w_pad)


def _lru_kernel(ax_ref, ag_ref, cinit_ref, h0_ref, cw_ref, cb_ref, wr_ref, br_ref, wi_ref, bi_ref,
                lam_ref, oa_ref, hlast_ref, hist_sc, h_sc, *, sb, c):
    ci = pl.program_id(1)

    @pl.when(ci == 0)
    def _():
        hist_sc[...] = cinit_ref[...]
        h_sc[...] = h0_ref[...]

    w = LRU_WIDTH
    u = ax_ref[...]
    ext = jnp.concatenate([hist_sc[...], u], axis=1).reshape(sb * (SUBLANES + c), w)
    acc = cb_ref[...] + ext * cw_ref[3:4, :]
    for j in range(1, 4):
        acc = acc + pltpu.roll(ext, j, 0) * cw_ref[3 - j:4 - j, :]
    xc = acc.reshape(sb, SUBLANES + c, w)[:, SUBLANES:, :].reshape(sb * c, w)
    hist_sc[...] = u[:, c - SUBLANES:, :]

    xb = xc.astype(BF16)
    half = w // 2

    def gate(w_ref, b_ref):
        pre = jnp.concatenate([_dot(xb[:, :half], w_ref[0]), _dot(xb[:, half:], w_ref[1])], axis=1)
        return _sigmoid(pre + b_ref[...])

    r = gate(wr_ref, br_ref)
    i = gate(wi_ref, bi_ref)
    log_a = -LRU_C * r * _softplus(-lam_ref[...])
    a = jnp.exp(log_a)
    b = jnp.sqrt(1.0 - a * a) * (i * xc)

    t = _mod_pow2(lax.broadcasted_iota(jnp.int32, (sb * c, w), 0), c)
    s = 1
    while s < c:
        m = t >= s
        a_sh = pltpu.roll(a, s, 0)
        b_sh = pltpu.roll(b, s, 0)
        b = jnp.where(m, a * b_sh + b, b)
        a = jnp.where(m, a * a_sh, a)
        s *= 2

    h = a.reshape(sb, c, w) * h_sc[...] + b.reshape(sb, c, w)
    h_sc[...] = h[:, c - 1:c, :]
    ag = ag_ref[...]
    oa_ref[...] = (h * (ag * _sigmoid(ag))).astype(BF16)

    @pl.when(ci == pl.num_programs(1) - 1)
    def _():
        hlast_ref[...] = h_sc[...]


def _lru(ax, ag, cinit, h0, p, *, sb, c):
    n_seq, t, w = ax.shape
    kern = functools.partial(_lru_kernel, sb=sb, c=c)
    blk = lambda s: pl.BlockSpec(s, lambda i, j: (i, j, 0))
    return pl.pallas_call(
        kern,
        grid=(n_seq // sb, t // c),
        in_specs=[
            blk((sb, c, w)), blk((sb, c, w)),
            pl.BlockSpec((sb, SUBLANES, w), lambda i, j: (i, 0, 0)),
            pl.BlockSpec((sb, 1, w), lambda i, j: (i, 0, 0)),
            _const_spec((SUBLANES, w)), _const_spec((1, w)),
            _const_spec((2, w // 2, w // 2)), _const_spec((1, w)),
            _const_spec((2, w // 2, w // 2)), _const_spec((1, w)),
            _const_spec((1, w)),
        ],
        out_specs=[blk((sb, c, w)), pl.BlockSpec((sb, 1, w), lambda i, j: (i, 0, 0))],
        out_shape=[jax.ShapeDtypeStruct((n_seq, t, w), BF16), jax.ShapeDtypeStruct((n_seq, 1, w), F32)],
        scratch_shapes=[pltpu.VMEM((sb, SUBLANES, w), F32), pltpu.VMEM((sb, 1, w), F32)],
        compiler_params=_cparams(("arbitrary", "arbitrary")),
        name="rg_lru",
    )(ax, ag, cinit, h0, p["conv_w8"], p["conv_b"], p["lru_wr2"], p["lru_br"], p["lru_wi2"],
      p["lru_bi"], p["lru_lam"])


def _rwkv_kernel(bs_ref, bg_ref, pinit_ref, s0_ref, mu_ref, w0_ref, lora_ref, a0_ref, kkp_ref,
                 ka_ref, rk_ref, gng_ref, gnb_ref, bones_ref, ob_ref, sout_ref, hist_sc, s_sc,
                 *, sb, c):
    ci = pl.program_id(1)
    rows = sb * c
    hw = RW_WIDTH

    @pl.when(ci == 0)
    def _():
        hist_sc[...] = pinit_ref[...]
        s_sc[...] = s0_ref[...]

    s3 = bs_ref[...]
    ext = jnp.concatenate([hist_sc[...], s3], axis=1).reshape(sb * (SUBLANES + c), SHIFT_COLS)
    sprev = pltpu.roll(ext, 1, 0).reshape(sb, SUBLANES + c, SHIFT_COLS)[:, SUBLANES:, :]
    sprev = sprev.reshape(rows, SHIFT_COLS)
    hist_sc[...] = s3[:, c - SUBLANES:, :]
    s = s3.reshape(rows, SHIFT_COLS)
    xs = s + (sprev - s) * mu_ref[...]
    r = xs[:, 0:hw]
    k = xs[:, hw:2 * hw]
    v = xs[:, 2 * hw:3 * hw]
    la = xs[:, 3 * hw:3 * hw + LANES]
    lane = lax.broadcasted_iota(jnp.int32, (rows, LANES), 1)
    low = lane < RW_HD
    lin = jnp.where(low, jnp.tanh(la), la).astype(BF16)
    lo = _dot(lin, lora_ref[...])
    wlog = -_softplus(-(w0_ref[...] + lo[:, :hw])) - 0.5
    logw = -jnp.exp(wlog)
    a = _sigmoid(a0_ref[...] + lo[:, hw:])
    bones = bones_ref[...]
    kk = k * kkp_ref[...]
    kk = kk * lax.rsqrt(_dot_sel_r(kk * kk, bones) + 1e-12)
    kf = k * (1.0 + (a - 1.0) * ka_ref[...])
    b = kk * a

    ri = lax.broadcasted_iota(jnp.int32, (rows, rows), 0)
    cj = lax.broadcasted_iota(jnp.int32, (rows, rows), 1)
    same_seq = _div_pow2(ri, c) == _div_pow2(cj, c)
    ltri = jnp.where(same_seq & (cj <= ri), 1.0, 0.0).astype(BF16)
    lseq = jnp.where(same_seq, 1.0, 0.0).astype(BF16)
    log_g = _dot_sel_l3(ltri, logw)
    log_end = _dot_sel_l3(lseq, logw)
    g_in = jnp.exp(log_g)
    g_ex = jnp.exp(log_g - logw)
    g_inv = jnp.exp(-log_g)
    g_rem = jnp.exp(log_end - log_g)
    g_end = jnp.exp(log_end)
    kt = kk * g_ex
    rt = r * g_in
    kh = kf * g_inv
    bh = b * g_inv
    kg = kf * g_rem
    bg_ = b * g_rem

    n2 = 2 * rows
    i2 = lax.broadcasted_iota(jnp.int32, (n2, n2), 0)
    j2 = lax.broadcasted_iota(jnp.int32, (n2, n2), 1)
    same_blk = _div_pow2(i2, c) == _div_pow2(j2, c)
    m_strict = same_blk & (j2 < i2)
    m_incl = same_blk & (j2 <= i2)
    eye = jnp.where(i2 == j2, 1.0, 0.0)

    def stack(x):
        return jnp.concatenate([jnp.where(low, x, 0.0), jnp.where(low, 0.0, x)], axis=0)

    def seqrows(x, q):
        return jnp.concatenate([x[q * c:(q + 1) * c], x[rows + q * c:rows + (q + 1) * c]], axis=0)

    def unseq(parts):
        return jnp.concatenate([x[:c] for x in parts] + [x[c:] for x in parts], axis=0)

    o_cols = []
    for pi in range(RW_PAIRS):
        col = slice(pi * LANES, (pi + 1) * LANES)
        xk = stack(kt[:, col])
        xr = stack(rt[:, col])
        vst = stack(v[:, col])
        kgs = stack(kg[:, col])
        bgs = stack(bg_[:, col])
        yb = jnp.concatenate([bh[:, col], bh[:, col]], axis=0).astype(BF16)
        yk = jnp.concatenate([kh[:, col], kh[:, col]], axis=0).astype(BF16)
        xx = jnp.concatenate([xk, xr], axis=0).astype(BF16)
        ab = _dot_nt(xx, yb)
        ak = _dot_nt(xx, yk)
        a_bk = jnp.where(m_strict, ab[:n2], 0.0)
        a_br = jnp.where(m_incl, ab[n2:], 0.0)
        a_kk = jnp.where(m_strict, ak[:n2], 0.0)
        a_kr = jnp.where(m_incl, ak[n2:], 0.0)

        pw = -a_bk
        tinv = eye + pw
        for _ in range(int(math.log2(c)) - 1):
            pw = _dot_x3(pw, pw)
            tinv = tinv + _dot_x3(tinv, pw)

        xs_k, xs_r = [], []
        for q in range(sb):
            lhs = jnp.concatenate([seqrows(xk, q), seqrows(xr, q)], axis=0).astype(BF16)
            res = _dot_nt(lhs, s_sc[q, pi].astype(BF16))
            xs_k.append(res[:2 * c])
            xs_r.append(res[2 * c:])
        xks = unseq(xs_k)
        xrs = unseq(xs_r)
        vb = vst.astype(BF16)
        uu = _dot_x3(tinv, xks + _dot(a_kk.astype(BF16), vb))
        ub = uu.astype(BF16)
        ost = xrs + _dot(a_kr.astype(BF16), vb) - _dot(a_br.astype(BF16), ub)
        o_cols.append(ost[:rows] + ost[rows:])

        for q in range(sb):
            lhs = jnp.concatenate([seqrows(vst, q), -seqrows(uu, q)], axis=0).astype(BF16)
            rhs = jnp.concatenate([seqrows(kgs, q), seqrows(bgs, q)], axis=0).astype(BF16)
            ds_ = _dot_tn(lhs, rhs)
            s_sc[q, pi] = s_sc[q, pi] * g_end[q * c:q * c + 1, col] + ds_

    o = jnp.concatenate(o_cols, axis=1)
    inv_hd = 1.0 / RW_HD
    mean = _dot_sel_r(o, bones) * inv_hd
    d = o - mean
    var = _dot_sel_r(d * d, bones) * inv_hd
    on = d * lax.rsqrt(var + GN_EPS) * gng_ref[...] + gnb_ref[...]
    bonus = _dot_sel_r(r * kf * rk_ref[...], bones) * v
    g = bg_ref[...].reshape(rows, hw)
    ob_ref[...] = ((on + bonus) * (g * _sigmoid(g))).astype(BF16).reshape(sb, c, hw)

    @pl.when(ci == pl.num_programs(1) - 1)
    def _():
        sout_ref[...] = s_sc[...]


def _rwkv(bs, bg, pinit, s0, p, *, sb, c):
    n_seq, t, _ = bs.shape
    hw = RW_WIDTH
    kern = functools.partial(_rwkv_kernel, sb=sb, c=c)
    blk = lambda s: pl.BlockSpec(s, lambda i, j: (i, j, 0))
    st_spec = pl.BlockSpec((sb, RW_PAIRS, LANES, LANES), lambda i, j: (i, 0, 0, 0))
    return pl.pallas_call(
        kern,
        grid=(n_seq // sb, t // c),
        in_specs=[
            blk((sb, c, SHIFT_COLS)), blk((sb, c, hw)),
            pl.BlockSpec((sb, SUBLANES, SHIFT_COLS), lambda i, j: (i, 0, 0)),
            st_spec,
            _const_spec((1, SHIFT_COLS)), _const_spec((1, hw)), _const_spec((LANES, 2 * hw)),
            _const_spec((1, hw)), _const_spec((1, hw)), _const_spec((1, hw)), _const_spec((1, hw)),
            _const_spec((1, hw)), _const_spec((1, hw)), _const_spec((hw, hw)),
        ],
        out_specs=[blk((sb, c, hw)), st_spec],
        out_shape=[jax.ShapeDtypeStruct((n_seq, t, hw), BF16),
                   jax.ShapeDtypeStruct((n_seq, RW_PAIRS, LANES, LANES), F32)],
        scratch_shapes=[pltpu.VMEM((sb, SUBLANES, SHIFT_COLS), F32),
                        pltpu.VMEM((sb, RW_PAIRS, LANES, LANES), F32)],
        compiler_params=_cparams(("arbitrary", "arbitrary")),
        name="rwkv7",
    )(bs, bg, pinit, s0, p["rw_mu"], p["rw_w0"], p["rw_lora"], p["rw_a0"], p["rw_kk"], p["rw_ka"],
      p["rw_rk"], p["rw_gn_g"], p["rw_gn_b"], p["bones"])


def _rope128(x, cos, sin_a, sin_b):
    return x * cos + pltpu.roll(x, LANES - ROPE_DIM // 2, 1) * sin_a + pltpu.roll(x, ROPE_DIM // 2, 1) * sin_b


def _mla_prep_kernel(cq_ref, ckv_ref, pe_ref, cos_ref, sa_ref, sb_ref, gqa_ref, wq_ref, gq_ref,
                     gkva_ref, wk_ref, gk_ref, wv_ref, ckvn_ref, q_ref, k_ref, v_ref, *, sb, tt):
    rows = sb * tt
    cq = cq_ref[...].reshape(rows, Q_RANK)
    cqn = (cq * lax.rsqrt(jnp.mean(cq * cq, axis=-1, keepdims=True) + EPS) * gqa_ref[...]).astype(BF16)
    q = _dot(cqn, wq_ref[...])
    ckv = ckv_ref[...].reshape(rows, KV_RANK)
    ckvn = ckv * lax.rsqrt(jnp.mean(ckv * ckv, axis=-1, keepdims=True) + EPS) * gkva_ref[...]
    ckvn_ref[...] = ckvn.reshape(sb, tt, KV_RANK)
    cb = ckvn.astype(BF16)
    kvk = _dot(cb, wk_ref[...])
    vv = _dot(cb, wv_ref[...])
    pe = pe_ref[...].reshape(rows, LANES)

    def tile(ref):
        return jnp.broadcast_to(ref[...][None], (sb, tt, LANES)).reshape(rows, LANES)

    cos, sa, sb_ = tile(cos_ref), tile(sa_ref), tile(sb_ref)
    lane = lax.broadcasted_iota(jnp.int32, (rows, LANES), 1)
    low = lane < V_DIM
    inv_d = 1.0 / QK_DIM
    for h in range(MLA_HEADS):
        col = slice(h * LANES, (h + 1) * LANES)
        qh = q[:, col]
        qn = qh * lax.rsqrt(jnp.sum(qh * qh, axis=-1, keepdims=True) * inv_d + EPS) * gq_ref[...]
        q_ref[:, h] = _rope128(qn, cos, sa, sb_).astype(BF16).reshape(sb, tt, LANES)
        kh = kvk[:, col] + pe
        kn = kh * lax.rsqrt(jnp.sum(kh * kh, axis=-1, keepdims=True) * inv_d + EPS) * gk_ref[...]
        k_ref[:, h] = _rope128(kn, cos, sa, sb_).astype(BF16).reshape(sb, tt, LANES)
        vp = vv[:, (h // 2) * LANES:(h // 2 + 1) * LANES]
        vm = jnp.where(low, vp, 0.0) if h % 2 == 0 else jnp.where(low, 0.0, vp)
        v_ref[:, h] = vm.astype(BF16).reshape(sb, tt, LANES)


def _mla_prep(cq, ckv, pe_a, tabs, p, *, sb, tt):
    n_seq, t, _ = cq.shape
    kern = functools.partial(_mla_prep_kernel, sb=sb, tt=tt)
    blk = lambda w: pl.BlockSpec((sb, tt, w), lambda i, j: (i, j, 0))
    tab = pl.BlockSpec((tt, LANES), lambda i, j: (j, 0))
    hd = pl.BlockSpec((sb, MLA_HEADS, tt, LANES), lambda i, j: (i, 0, j, 0))
    hshape = jax.ShapeDtypeStruct((n_seq, MLA_HEADS, t, LANES), BF16)
    return pl.pallas_call(
        kern,
        grid=(n_seq // sb, t // tt),
        in_specs=[
            blk(Q_RANK), blk(KV_RANK), blk(LANES), tab, tab, tab,
            _const_spec((1, Q_RANK)), _const_spec((Q_RANK, MLA_HEADS * LANES)), _const_spec((1, LANES)),
            _const_spec((1, KV_RANK)), _const_spec((KV_RANK, MLA_HEADS * LANES)), _const_spec((1, LANES)),
            _const_spec((KV_RANK, MLA_WIDTH)),
        ],
        out_specs=[blk(KV_RANK), hd, hd, hd],
        out_shape=[jax.ShapeDtypeStruct((n_seq, t, KV_RANK), F32), hshape, hshape, hshape],
        compiler_params=_cparams(("arbitrary", "arbitrary")),
        name="mla_prep",
    )(cq, ckv, pe_a, tabs[0], tabs[1], tabs[2], p["g_qa"], p["wq"], p["gq128"], p["g_kva"],
      p["wk"], p["gk128"], p["wv"])


def _flash_kernel(q_ref, k_ref, v_ref, o_ref, *, tq):
    qi = pl.program_id(2)
    row = lax.broadcasted_iota(jnp.int32, (tq, tq), 0)
    colj = lax.broadcasted_iota(jnp.int32, (tq, tq), 1)
    causal = colj <= row
    out = None
    for hh in range(2):
        q = q_ref[0, hh]

        def step(j, carry, diag, hh=hh, q=q):
            m, l, acc = carry
            kj = k_ref[0, hh, pl.ds(j * tq, tq), :]
            vj = v_ref[0, hh, pl.ds(j * tq, tq), :]
            s = _dot_nt(q, kj) * SCALE
            if diag:
                s = jnp.where(causal, s, NEG)
            m_new = jnp.maximum(m, jnp.max(s, axis=-1, keepdims=True))
            pr = jnp.exp(s - m_new)
            corr = jnp.exp(m - m_new)
            l = l * corr + jnp.sum(pr, axis=-1, keepdims=True)
            acc = acc * corr + _dot(pr.astype(BF16), vj)
            return m_new, l, acc

        init = (jnp.full((tq, 1), NEG, F32), jnp.zeros((tq, 1), F32), jnp.zeros((tq, LANES), F32))
        carry = lax.fori_loop(0, qi, functools.partial(step, diag=False), init)
        _, l, acc = step(qi, carry, True)
        oh = acc / l
        out = oh if out is None else out + oh
    o_ref[0] = out.astype(BF16)


def _flash(q, k, v, *, tq):
    b, h, t, _ = q.shape
    kern = functools.partial(_flash_kernel, tq=tq)
    return pl.pallas_call(
        kern,
        grid=(b, h // 2, t // tq),
        in_specs=[
            pl.BlockSpec((1, 2, tq, LANES), lambda bi, pi, qi: (bi, pi, qi, 0)),
            pl.BlockSpec((1, 2, t, LANES), lambda bi, pi, qi: (bi, pi, 0, 0)),
            pl.BlockSpec((1, 2, t, LANES), lambda bi, pi, qi: (bi, pi, 0, 0)),
        ],
        out_specs=pl.BlockSpec((1, tq, LANES), lambda bi, pi, qi: (bi, qi, pi)),
        out_shape=jax.ShapeDtypeStruct((b, t, (h // 2) * LANES), BF16),
        compiler_params=_cparams(("arbitrary", "arbitrary", "arbitrary")),
        name="mla_flash",
    )(q, k, v)


def _decode_kernel(pt_ref, q_ref, cnew_ref, penew_ref, ckv_hbm, kpe_hbm, wukt_ref, gkn_ref, sel_ref,
                   wukp_ref, e16_ref, gkr_ref, ti_ref, tp_ref, tn_ref, wuv_ref, o_ref,
                   cbuf, pbuf, sem, *, layer, pc, n_chunks, ts):
    b = pl.program_id(0)
    nb = pl.num_programs(0)
    page = ckv_hbm.shape[2]
    tok = pc * page
    hq = MLA_HEADS * ts

    def copies(bb, ch, slot):
        out = []
        for i in range(pc):
            phys = pt_ref[bb, ch * pc + i]
            out.append(pltpu.make_async_copy(
                ckv_hbm.at[layer, phys], cbuf.at[slot, pl.ds(i * page, page), :], sem.at[0, slot]))
            out.append(pltpu.make_async_copy(
                kpe_hbm.at[layer, phys], pbuf.at[slot, pl.ds(i * page, page), :], sem.at[1, slot]))
        return out

    @pl.when(b == 0)
    def _():
        for cp in copies(0, 0, 0):
            cp.start()

    qf = q_ref[0].astype(F32).reshape(hq, LANES)
    qg = qf * gkn_ref[...]
    qlat = jnp.concatenate(
        [_dot(qg[h * ts:(h + 1) * ts].astype(BF16), wukt_ref[h]) for h in range(MLA_HEADS)],
        axis=0).astype(BF16)
    qrope = _dot(qf.astype(BF16), sel_ref[...]).astype(BF16)
    e16 = e16_ref[...]
    ones_r = jnp.ones((hq, LANES), BF16)
    inv_d = 1.0 / QK_DIM

    def attend(carry, ck, kr, pe_raw, mask):
        m, l, acc = carry
        cb = ck.astype(BF16)
        kn = _dot(cb, wukp_ref[...])
        part = kn[:, 0:LANES] * kn[:, 0:LANES]
        for cc in range(1, MLA_HEADS * NOPE_DIM // LANES):
            kc = kn[:, cc * LANES:(cc + 1) * LANES]
            part = part + kc * kc
        ssq = _dot_nt(e16, part.astype(BF16)) + _dot_nt(ones_r, (pe_raw * pe_raw).astype(BF16))
        s = _dot_nt(qlat, cb) + _dot_nt(qrope, kr.astype(BF16))
        s = s * lax.rsqrt(ssq * inv_d + EPS) * SCALE
        if mask is not None:
            s = jnp.where(mask, s, NEG)
        m_new = jnp.maximum(m, jnp.max(s, axis=-1, keepdims=True))
        pr = jnp.exp(s - m_new)
        corr = jnp.exp(m - m_new)
        l = l * corr + jnp.sum(pr, axis=-1, keepdims=True)
        acc = acc * corr + _dot(pr.astype(BF16), cb)
        return m_new, l, acc

    def chunk(ch, carry):
        slot = (b * n_chunks + ch) % 2
        for cp in copies(b, ch, slot):
            cp.wait()

        @pl.when(ch + 1 < n_chunks)
        def _():
            for cp in copies(b, ch + 1, 1 - slot):
                cp.start()

        @pl.when((ch + 1 == n_chunks) & (b + 1 < nb))
        def _():
            for cp in copies(b + 1, 0, 1 - slot):
                cp.start()

        ck = cbuf[slot]
        pe_raw = jnp.concatenate([pbuf[slot], jnp.zeros((tok, LANES - ROPE_DIM), F32)], axis=1)
        x = pe_raw * gkr_ref[...]
        parts = []
        for i in range(pc):
            pg = ch * pc + i
            yi = _rope128(x[i * page:(i + 1) * page], ti_ref[0], ti_ref[1], ti_ref[2])
            parts.append(_rope128(yi, tp_ref[0, pl.ds(pg, 1), :], tp_ref[1, pl.ds(pg, 1), :],
                                  tp_ref[2, pl.ds(pg, 1), :]))
        kr = jnp.concatenate(parts, axis=0)
        return attend(carry, ck, kr, pe_raw, None)

    init = (jnp.full((hq, 1), NEG, F32), jnp.zeros((hq, 1), F32), jnp.zeros((hq, LANES), F32))
    carry = lax.fori_loop(0, n_chunks, chunk, init)

    pad = jnp.zeros((page - ts, LANES), F32)
    cn = jnp.concatenate([cnew_ref[0], pad], axis=0)
    pn = jnp.concatenate([penew_ref[0], pad], axis=0)
    krn = _rope128(pn * gkr_ref[...], tn_ref[0], tn_ref[1], tn_ref[2])
    qidx = _mod_pow2(lax.broadcasted_iota(jnp.int32, (hq, page), 0), ts)
    tidx = lax.broadcasted_iota(jnp.int32, (hq, page), 1)
    _, l, acc = attend(carry, cn, krn, pn, tidx <= qidx)
    al = acc / l
    out = _dot(al[0:ts].astype(BF16), wuv_ref[0])
    for h in range(1, MLA_HEADS):
        out = out + _dot(al[h * ts:(h + 1) * ts].astype(BF16), wuv_ref[h])
    o_ref[0] = out


def _decode(page_table, q, cnew, penew, cache_ckv, cache_kpe, tabs_i, tabs_p, tabs_n, p, *, layer, pc):
    db, h, ts, _ = q.shape
    n_pages = page_table.shape[1]
    page = cache_ckv.shape[2]
    n_chunks = n_pages // pc
    tok = pc * page
    kern = functools.partial(_decode_kernel, layer=layer, pc=pc, n_chunks=n_chunks, ts=ts)
    cs = lambda shape: pl.BlockSpec(shape, lambda b, pt: (0,) * len(shape))
    grid_spec = pltpu.PrefetchScalarGridSpec(
        num_scalar_prefetch=1,
        grid=(db,),
        in_specs=[
            pl.BlockSpec((1, h, ts, LANES), lambda b, pt: (b, 0, 0, 0)),
            pl.BlockSpec((1, ts, LANES), lambda b, pt: (b, 0, 0)),
            pl.BlockSpec((1, ts, LANES), lambda b, pt: (b, 0, 0)),
            pl.BlockSpec(memory_space=pl.ANY),
            pl.BlockSpec(memory_space=pl.ANY),
            cs((h, LANES, LANES)), cs((1, LANES)), cs((LANES, LANES)),
            cs((KV_RANK, h * NOPE_DIM)), cs((h * ts, LANES)), cs((1, LANES)),
            cs((3, page, LANES)), cs((3, n_pages, LANES)), cs((3, page, LANES)),
            cs((h, KV_RANK, MLA_WIDTH)),
        ],
        out_specs=pl.BlockSpec((1, ts, MLA_WIDTH), lambda b, pt: (b, 0, 0)),
        scratch_shapes=[
            pltpu.VMEM((2, tok, KV_RANK), F32),
            pltpu.VMEM((2, tok, ROPE_DIM), F32),
            pltpu.SemaphoreType.DMA((2, 2)),
        ],
    )
    return pl.pallas_call(
        kern,
        grid_spec=grid_spec,
        out_shape=jax.ShapeDtypeStruct((db, ts, MLA_WIDTH), F32),
        compiler_params=_cparams(("arbitrary",)),
        name="mla_decode",
    )(page_table, q, cnew, penew, cache_ckv, cache_kpe, p["wukt"], p["gkn128"], p["sel_rope"],
      p["wukp"], p["e16"], p["gkr128"], tabs_i, tabs_p, tabs_n, p["wuv"])


def _merge_kernel(x_ref, gate_ref, oa_ref, ob_ref, oc_ref, cg_ref, woa_ref, wob_ref, woc_ref,
                  wout_ref, y_ref):
    d = x_ref.shape[1]
    cg = cg_ref[...]
    oc = (oc_ref[...].astype(F32) * (cg * _sigmoid(cg))).astype(BF16)
    g = _sigmoid(gate_ref[...])
    merged = (g[:, 0:d] * _dot(oa_ref[...], woa_ref[...])
              + g[:, d:2 * d] * _dot(ob_ref[...], wob_ref[...])
              + g[:, 2 * d:3 * d] * _dot(oc, woc_ref[...]))
    y_ref[...] = x_ref[...] + _dot(merged.astype(BF16), wout_ref[...])


def _merge(x2d, gate, oa, ob, oc, cg, p):
    n, d = x2d.shape
    tm = min(512, n)
    row = lambda w: pl.BlockSpec((tm, w), lambda i: (i, 0))
    return pl.pallas_call(
        _merge_kernel,
        grid=(n // tm,),
        in_specs=[row(d), row(3 * d), row(LRU_WIDTH), row(RW_WIDTH), row(MLA_WIDTH), row(MLA_WIDTH),
                  _const_spec((LRU_WIDTH, d)), _const_spec((RW_WIDTH, d)), _const_spec((MLA_WIDTH, d)),
                  _const_spec((d, d))],
        out_specs=row(d),
        out_shape=jax.ShapeDtypeStruct((n, d), F32),
        compiler_params=_cparams(("arbitrary",)),
        name="merge",
    )(x2d, gate, oa, ob, oc, cg, p["w_oa"], p["w_ob"], p["w_oc"], p["w_out"])


def _rope_pattern(pos, lane0, rows=None):
    half = ROPE_DIM // 2
    inv = 1.0 / (ROPE_THETA ** (jnp.arange(0, ROPE_DIM, 2, dtype=F32) / ROPE_DIM))
    ang = pos.astype(F32)[:, None] * inv[None, :]
    cos, sin = jnp.cos(ang), jnp.sin(ang)
    n = pos.shape[0]
    ones_before = jnp.ones((n, lane0), F32) if lane0 else jnp.zeros((n, 0), F32)
    z = lambda w: jnp.zeros((n, w), F32)
    rest = LANES - lane0 - ROPE_DIM
    cos_t = jnp.concatenate([ones_before, cos, cos, z(rest)], axis=1)
    sin_a = jnp.concatenate([z(lane0), -sin, z(half), z(rest)], axis=1)
    sin_b = jnp.concatenate([z(lane0), z(half), sin, z(rest)], axis=1)
    tabs = jnp.stack([cos_t, sin_a, sin_b], axis=0)
    if rows is not None and rows > n:
        tabs = jnp.pad(tabs, ((0, 0), (0, rows - n), (0, 0)))
    return tabs


def _block_diag2(w):
    nb, bs, _ = w.shape
    per = nb // 2
    out = jnp.zeros((2, per * bs, per * bs), w.dtype)
    for i in range(nb):
        hf, j = divmod(i, per)
        out = out.at[hf, j * bs:(j + 1) * bs, j * bs:(j + 1) * bs].set(w[i])
    return out


def _layer_params(l, ts, a):
    d = a["w_in"].shape[1]
    w = a["w_in"][l]
    pe0 = SEG_CKV[1] - 0
    src_pe = 3616 - 32
    z = lambda n: jnp.zeros((d, n), w.dtype)
    w_pad = jnp.concatenate([
        w[:, :src_pe],
        z(64), w[:, src_pe:src_pe + ROPE_DIM], z(32),
        w[:, src_pe:src_pe + ROPE_DIM], z(96),
        w[:, src_pe + ROPE_DIM:],
    ], axis=1).astype(BF16)
    assert w_pad.shape[1] == IN_COLS_PAD and pe0 == SEG_PEA[0]
    row = lambda v: v.reshape(1, -1)
    hd = RW_HD
    bones = jnp.kron(jnp.eye(RW_WIDTH // hd, dtype=F32), jnp.ones((hd, hd), F32)).astype(BF16)
    zl = jnp.zeros((LORA, RW_WIDTH), F32)
    lora = jnp.concatenate([jnp.concatenate([a["rw_w2"][l], zl], axis=1),
                            jnp.concatenate([zl, a["rw_a2"][l]], axis=1)], axis=0).astype(BF16)
    w_uq = a["mla_w_uq"][l]
    w_ukv = a["mla_w_ukv"][l]
    w_uk, w_uv = w_ukv[..., :NOPE_DIM], w_ukv[..., NOPE_DIM:]
    g_kn = a["mla_g_kn"][l]
    pad_h = lambda x: jnp.pad(x, ((0, 0), (0, 0), (0, LANES - x.shape[-1]))).reshape(x.shape[0], -1)
    wukt = jnp.pad(jnp.transpose(w_uk, (1, 2, 0)), ((0, 0), (0, LANES - NOPE_DIM), (0, 0)))
    wukp = jnp.transpose(w_uk.reshape(KV_RANK, MLA_HEADS, NOPE_DIM // 16, 16), (0, 2, 1, 3))
    wukp = wukp.reshape(KV_RANK, MLA_HEADS * NOPE_DIM)
    e16 = jnp.repeat(jnp.kron(jnp.eye(MLA_HEADS, dtype=F32), jnp.ones((1, 16), F32)), ts, axis=0)
    wuv = jnp.zeros((MLA_HEADS, KV_RANK, MLA_WIDTH), F32)
    for h in range(MLA_HEADS):
        wuv = wuv.at[h, :, h * V_DIM:(h + 1) * V_DIM].set(w_uv[:, h, :])
    sel = jnp.zeros((LANES, LANES), F32).at[
        jnp.arange(NOPE_DIM, QK_DIM), jnp.arange(ROPE_DIM)].set(1.0)
    return {
        "norm_g": row(a["norm_g"][l]), "w_in": w_pad,
        "conv_w8": jnp.pad(a["conv_w"][l], ((0, SUBLANES - a["conv_w"].shape[1]), (0, 0))),
        "conv_b": row(a["conv_b"][l]),
        "lru_wr2": _block_diag2(a["lru_wr"][l]).astype(BF16), "lru_br": row(a["lru_br"][l]),
        "lru_wi2": _block_diag2(a["lru_wi"][l]).astype(BF16), "lru_bi": row(a["lru_bi"][l]),
        "lru_lam": row(a["lru_lam"][l]),
        "rw_mu": row(a["rw_mu"][l]), "rw_w0": row(a["rw_w0"][l]), "rw_lora": lora,
        "rw_a0": row(a["rw_a0"][l]), "rw_kk": row(a["rw_kk"][l]), "rw_ka": row(a["rw_ka"][l]),
        "rw_rk": row(a["rw_rk"][l]), "rw_gn_g": row(a["rw_gn_g"][l]), "rw_gn_b": row(a["rw_gn_b"][l]),
        "bones": bones,
        "g_qa": row(a["mla_g_qa"][l]), "wq": pad_h(w_uq).astype(BF16),
        "gq128": row(jnp.pad(a["mla_g_qn"][l], (0, LANES - QK_DIM))),
        "g_kva": row(a["mla_g_kva"][l]), "wk": pad_h(w_uk).astype(BF16),
        "gk128": row(jnp.pad(g_kn, (0, LANES - QK_DIM))),
        "wv": w_uv.reshape(KV_RANK, MLA_WIDTH).astype(BF16),
        "wukt": wukt.astype(BF16),
        "gkn128": row(jnp.pad(g_kn[:NOPE_DIM], (0, LANES - NOPE_DIM))),
        "sel_rope": sel.astype(BF16), "wukp": wukp.astype(BF16), "e16": e16.astype(BF16),
        "gkr128": row(jnp.pad(g_kn[NOPE_DIM:], (0, LANES - ROPE_DIM))),
        "wuv": wuv.astype(BF16),
        "w_oa": a["w_oa"][l].astype(BF16), "w_ob": a["w_ob"][l].astype(BF16),
        "w_oc": a["w_oc"][l].astype(BF16), "w_out": a["w_out"][l].astype(BF16),
    }


def _state_to_pairs(s):
    n = s.shape[0]
    s = s.reshape(n, RW_PAIRS, 2, RW_HD, RW_HD)
    z = jnp.zeros_like(s[:, :, 0])
    top = jnp.concatenate([s[:, :, 0], z], axis=-1)
    bot = jnp.concatenate([z, s[:, :, 1]], axis=-1)
    return jnp.concatenate([top, bot], axis=-2)


def _pairs_to_state(sp):
    n = sp.shape[0]
    a = sp[:, :, :RW_HD, :RW_HD]
    b = sp[:, :, RW_HD:, RW_HD:]
    return jnp.stack([a, b], axis=2).reshape(n, 2 * RW_PAIRS, RW_HD, RW_HD)


def _hist_rows(x):
    return jnp.pad(x, ((0, 0), (SUBLANES - x.shape[1], 0), (0, 0)))


def _mixer_layer(x, p, lru_h0, conv_buf, rw_s0, rw_prev, tabs, attend, *, lru_blk, rw_blk, prep_blk):
    n_seq, t, d = x.shape
    n = n_seq * t
    segs = _inproj(x.reshape(n, d), p["norm_g"], p["w_in"])
    ax, ag, bs, bg, cq, ckv_raw, pe_a, pe_b, cg, gate = segs
    r3 = lambda v: v.reshape(n_seq, t, v.shape[-1])
    oa, lru_h = _lru(r3(ax), r3(ag), _hist_rows(conv_buf), lru_h0[:, None, :], p, sb=lru_blk[0],
                     c=lru_blk[1])
    lru_h = lru_h[:, 0, :]
    ob, s_pairs = _rwkv(r3(bs), r3(bg), _hist_rows(rw_prev[:, None, :]), _state_to_pairs(rw_s0), p,
                        sb=rw_blk[0], c=rw_blk[1])
    ckv, q, k, v = _mla_prep(r3(cq), r3(ckv_raw), r3(pe_a), tabs, p, sb=prep_blk[0], tt=prep_blk[1])
    oc = attend(q, k, v, ckv, r3(pe_b))
    y = _merge(x.reshape(n, d), gate, oa.reshape(n, -1), ob.reshape(n, -1), oc.reshape(n, -1), cg, p)
    conv_new = r3(ax)[:, t - conv_buf.shape[1]:, :]
    shift_new = r3(bs)[:, t - 1, :]
    kpe = r3(pe_b)[:, :, :ROPE_DIM]
    return y.reshape(n_seq, t, d), (lru_h, conv_new, _pairs_to_state(s_pairs), shift_new, ckv, kpe)


def kernel(x_prompt, x_sample, state_lru_h, state_lru_conv, state_rwkv_S, state_rwkv_shift, cache_ckv, cache_kpe, page_table, norm_g, w_in, conv_w, conv_b, lru_wr, lru_br, lru_wi, lru_bi, lru_lam, rw_mu, rw_w0, rw_w2, rw_a0, rw_a2, rw_kk, rw_ka, rw_rk, rw_gn_g, rw_gn_b, mla_g_qa, mla_g_kva, mla_w_uq, mla_w_ukv, mla_g_qn, mla_g_kn, w_oa, w_ob, w_oc, w_out):
    a = dict(norm_g=norm_g, w_in=w_in, conv_w=conv_w, conv_b=conv_b, lru_wr=lru_wr, lru_br=lru_br,
             lru_wi=lru_wi, lru_bi=lru_bi, lru_lam=lru_lam, rw_mu=rw_mu, rw_w0=rw_w0, rw_w2=rw_w2,
             rw_a0=rw_a0, rw_a2=rw_a2, rw_kk=rw_kk, rw_ka=rw_ka, rw_rk=rw_rk, rw_gn_g=rw_gn_g,
             rw_gn_b=rw_gn_b, mla_g_qa=mla_g_qa, mla_g_kva=mla_g_kva, mla_w_uq=mla_w_uq,
             mla_w_ukv=mla_w_ukv, mla_g_qn=mla_g_qn, mla_g_kn=mla_g_kn, w_oa=w_oa, w_ob=w_ob,
             w_oc=w_oc, w_out=w_out)
    bsz, t, _ = x_prompt.shape
    db, ts, _ = x_sample.shape
    depth = w_in.shape[0]
    n_pages = page_table.shape[1]
    page = cache_ckv.shape[2]
    past = n_pages * page
    dt = x_prompt.dtype
    assert ts == SUBLANES and t % RW_BLOCK_ROWS == 0 and db % SUBLANES == 0

    tabs_p = _rope_pattern(jnp.arange(t), NOPE_DIM)
    tabs_s = _rope_pattern(past + jnp.arange(ts), NOPE_DIM)
    dec_i = _rope_pattern(jnp.arange(page), 0)
    dec_p = _rope_pattern(jnp.arange(n_pages) * page, 0)
    dec_n = _rope_pattern(past + jnp.arange(ts), 0, rows=page)

    tq = min(512, t)
    pc = min(8, n_pages)
    lru_c = min(256, t)
    prep_tt = min(512, t)
    sb_s = min(32, db)

    hp, hs = x_prompt, x_sample
    st_p, st_s = [], []
    for l in range(depth):
        p = _layer_params(l, ts, a)

        def attend_prompt(q, k, v, ckv, pe_b):
            return _flash(q, k, v, tq=tq)

        def attend_sample(q, k, v, ckv, pe_b, l=l, p=p):
            return _decode(page_table, q, ckv, pe_b, cache_ckv, cache_kpe, dec_i, dec_p, dec_n, p,
                           layer=l, pc=pc)

        hp, sp = _mixer_layer(
            hp, p, jnp.zeros((bsz, LRU_WIDTH), dt), jnp.zeros((bsz, 3, LRU_WIDTH), dt),
            jnp.zeros((bsz, 2 * RW_PAIRS, RW_HD, RW_HD), dt), jnp.zeros((bsz, SHIFT_COLS), dt),
            tabs_p, attend_prompt, lru_blk=(bsz, lru_c), rw_blk=(1, RW_BLOCK_ROWS),
            prep_blk=(1, prep_tt))
        hs, ss = _mixer_layer(
            hs, p, state_lru_h[l], state_lru_conv[l], state_rwkv_S[l], state_rwkv_shift[l],
            tabs_s, attend_sample, lru_blk=(sb_s, ts), rw_blk=(RW_BLOCK_ROWS // ts, ts),
            prep_blk=(sb_s, ts))
        st_p.append(sp)
        st_s.append(ss)

    stack = lambda states, i: jnp.stack([s[i] for s in states], axis=0)
    return (hp, hs) + tuple(stack(st_p, i) for i in range(6)) + tuple(stack(st_s, i) for i in range(6))
```

```python
import functools
import math

import jax
import jax.numpy as jnp
from jax import lax
from jax.experimental import pallas as pl
from jax.experimental.pallas import tpu as pltpu

F32 = jnp.float32
BF16 = jnp.bfloat16

EPS = 1e-6
LRU_C = 8.0
GN_EPS = 64e-5
ROPE_THETA = 10000.0
NEG = -1e30

LANES = 128
SUBLANES = 8
VMEM_LIMIT_BYTES = 56 * 1024 * 1024

LRU_WIDTH = 512
RW_WIDTH = 512
RW_HD = 64
RW_PAIRS = RW_WIDTH // LANES
LORA = 64
SHIFT_COLS = 3 * RW_WIDTH + 2 * LORA
Q_RANK = 256
KV_RANK = 128
MLA_HEADS = 8
NOPE_DIM = 64
ROPE_DIM = 32
V_DIM = 64
QK_DIM = NOPE_DIM + ROPE_DIM
MLA_WIDTH = MLA_HEADS * V_DIM
SCALE = QK_DIM ** -0.5

RW_TB = 64
FLASH_TQ = 2048
FLASH_TK = 512
DECODE_PAGES = 32
NEW_ROWS = 16

SEG_AX = (0, 512)
SEG_AG = (512, 1024)
SEG_BS = (1024, 2688)
SEG_BG = (2688, 3200)
SEG_CQ = (3200, 3456)
SEG_CKV = (3456, 3584)
SEG_PEA = (3584, 3712)
SEG_PEB = (3712, 3840)
SEG_CG = (3840, 4352)
SEG_GATE = (4352, 7424)
IN_SEGS = (SEG_AX, SEG_AG, SEG_BS, SEG_BG, SEG_CQ, SEG_CKV, SEG_PEA, SEG_PEB, SEG_CG, SEG_GATE)
IN_COLS_PAD = SEG_GATE[1]


def _cparams(sem):
    return pltpu.CompilerParams(dimension_semantics=sem, vmem_limit_bytes=VMEM_LIMIT_BYTES)


def _const_spec(shape):
    nd = len(shape)
    return pl.BlockSpec(shape, lambda *_: (0,) * nd)


def _sigmoid(x):
    return jax.nn.sigmoid(x)


def _softplus(x):
    return jnp.maximum(x, 0.0) + jnp.log(1.0 + jnp.exp(-jnp.abs(x)))


def _mod_pow2(x, n):
    return jnp.bitwise_and(x, n - 1)


def _div_pow2(x, n):
    return lax.shift_right_logical(x, int(math.log2(n)))


def _dot(a, b):
    return jnp.dot(a, b, preferred_element_type=F32)


def _dot_nt(a, b):
    return lax.dot_general(a, b, (((1,), (1,)), ((), ())), preferred_element_type=F32)


def _bdot(a, b):
    return jnp.einsum('gij,gjk->gik', a, b, preferred_element_type=F32)


def _bdot_nt(a, b):
    return jnp.einsum('gik,gjk->gij', a, b, preferred_element_type=F32)


def _bdot_tn(a, b):
    return jnp.einsum('gti,gtj->gij', a, b, preferred_element_type=F32)


def _split2(x):
    hi = x.astype(BF16)
    lo = (x - hi.astype(F32)).astype(BF16)
    return hi, lo


def _split3(x):
    hi = x.astype(BF16)
    r1 = x - hi.astype(F32)
    mid = r1.astype(BF16)
    lo = (r1 - mid.astype(F32)).astype(BF16)
    return hi, mid, lo


def _dot_sel_r(x, m01):
    hi, lo = _split2(x)
    return _dot(hi, m01) + _dot(lo, m01)


def _dot_sel_l3(m01, x):
    hi, mid, lo = _split3(x)
    return _dot(m01, hi) + _dot(m01, mid) + _dot(m01, lo)


def _bdot16(a, b):
    return _bdot(a.astype(BF16), b.astype(BF16))


def _inproj_kernel(x_ref, g_ref, w_ref, *out_refs):
    x = x_ref[...]
    ms = jnp.mean(x * x, axis=-1, keepdims=True)
    hn = (x * lax.rsqrt(ms + EPS) * g_ref[...]).astype(BF16)
    for o_ref, (lo, hi) in zip(out_refs, IN_SEGS):
        for c0 in range(lo, hi, 512):
            c1 = min(c0 + 512, hi)
            o_ref[:, c0 - lo:c1 - lo] = _dot(hn, w_ref[:, c0:c1])


def _inproj(x2d, g, w_pad):
    n, d = x2d.shape
    tm = min(256, n)
    out_shape = [jax.ShapeDtypeStruct((n, hi - lo), F32) for lo, hi in IN_SEGS]
    out_specs = [pl.BlockSpec((tm, hi - lo), lambda i: (i, 0)) for lo, hi in IN_SEGS]
    return pl.pallas_call(
        _inproj_kernel,
        grid=(n // tm,),
        in_specs=[
            pl.BlockSpec((tm, d), lambda i: (i, 0)),
            _const_spec((1, d)),
            pl.BlockSpec((d, IN_COLS_PAD), lambda i: (0, 0), pipeline_mode=pl.Buffered(1)),
        ],
        out_specs=out_specs,
        out_shape=out_shape,
        compiler_params=_cparams(("arbitrary",)),
        name="inproj",
    )(x2d, g, w_pad)


def _lru_kernel(ax_ref, ag_ref, cinit_ref, h0_ref, cw_ref, cb_ref, wr_ref, br_ref, wi_ref, bi_ref,
                lam_ref, oa_ref, hlast_ref, hist_sc, h_sc, *, sb, c):
    ci = pl.program_id(1)

    @pl.when(ci == 0)
    def _():
        hist_sc[...] = cinit_ref[...]
        h_sc[...] = h0_ref[...]

    w = LRU_WIDTH
    u = ax_ref[...]
    ext = jnp.concatenate([hist_sc[...], u], axis=1).reshape(sb * (SUBLANES + c), w)
    acc = cb_ref[...] + ext * cw_ref[3:4, :]
    for j in range(1, 4):
        acc = acc + pltpu.roll(ext, j, 0) * cw_ref[3 - j:4 - j, :]
    xc = acc.reshape(sb, SUBLANES + c, w)[:, SUBLANES:, :].reshape(sb * c, w)
    hist_sc[...] = u[:, c - SUBLANES:, :]

    xb = xc.astype(BF16)
    half = w // 2

    def gate(w_ref, b_ref):
        pre = jnp.concatenate([_dot(xb[:, :half], w_ref[0]), _dot(xb[:, half:], w_ref[1])], axis=1)
        return _sigmoid(pre + b_ref[...])

    r = gate(wr_ref, br_ref)
    i = gate(wi_ref, bi_ref)
    log_a = -LRU_C * r * _softplus(-lam_ref[...])
    a = jnp.exp(log_a)
    b = jnp.sqrt(1.0 - a * a) * (i * xc)

    t = _mod_pow2(lax.broadcasted_iota(jnp.int32, (sb * c, w), 0), c)
    s = 1
    while s < c:
        m = t >= s
        a_sh = pltpu.roll(a, s, 0)
        b_sh = pltpu.roll(b, s, 0)
        b = jnp.where(m, a * b_sh + b, b)
        a = jnp.where(m, a * a_sh, a)
        s *= 2

    h = a.reshape(sb, c, w) * h_sc[...] + b.reshape(sb, c, w)
    h_sc[...] = h[:, c - 1:c, :]
    ag = ag_ref[...]
    oa_ref[...] = (h * (ag * _sigmoid(ag))).astype(BF16)

    @pl.when(ci == pl.num_programs(1) - 1)
    def _():
        hlast_ref[...] = h_sc[...]


def _lru(ax, ag, cinit, h0, p, *, sb, c):
    n_seq, t, w = ax.shape
    kern = functools.partial(_lru_kernel, sb=sb, c=c)
    blk = lambda s: pl.BlockSpec(s, lambda i, j: (i, j, 0))
    return pl.pallas_call(
        kern,
        grid=(n_seq // sb, t // c),
        in_specs=[
            blk((sb, c, w)), blk((sb, c, w)),
            pl.BlockSpec((sb, SUBLANES, w), lambda i, j: (i, 0, 0)),
            pl.BlockSpec((sb, 1, w), lambda i, j: (i, 0, 0)),
            _const_spec((SUBLANES, w)), _const_spec((1, w)),
            _const_spec((2, w // 2, w // 2)), _const_spec((1, w)),
            _const_spec((2, w // 2, w // 2)), _const_spec((1, w)),
            _const_spec((1, w)),
        ],
        out_specs=[blk((sb, c, w)), pl.BlockSpec((sb, 1, w), lambda i, j: (i, 0, 0))],
        out_shape=[jax.ShapeDtypeStruct((n_seq, t, w), BF16), jax.ShapeDtypeStruct((n_seq, 1, w), F32)],
        scratch_shapes=[pltpu.VMEM((sb, SUBLANES, w), F32), pltpu.VMEM((sb, 1, w), F32)],
        compiler_params=_cparams(("arbitrary", "arbitrary")),
        name="rg_lru",
    )(ax, ag, cinit, h0, p["conv_w8"], p["conv_b"], p["lru_wr2"], p["lru_br"], p["lru_wi2"],
      p["lru_bi"], p["lru_lam"])


def _rwkv_kernel(bs_ref, bg_ref, pinit_ref, s0_ref, mu_ref, w0_ref, lora_ref, a0_ref, kkp_ref,
                 ka_ref, rk_ref, gng_ref, gnb_ref, bones_ref, ob_ref, sout_ref, hist_sc, s_sc,
                 *, sb, c):
    ci = pl.program_id(1)
    rows = sb * c
    hw = RW_WIDTH
    tb = RW_TB
    ntb = rows // tb
    spt = tb // c
    gt = RW_PAIRS * ntb
    gs = RW_PAIRS * sb

    @pl.when(ci == 0)
    def _():
        hist_sc[...] = pinit_ref[...]
        s_sc[...] = s0_ref[...]

    s3 = bs_ref[...]
    ext = jnp.concatenate([hist_sc[...], s3], axis=1).reshape(sb * (SUBLANES + c), SHIFT_COLS)
    sprev = pltpu.roll(ext, 1, 0).reshape(sb, SUBLANES + c, SHIFT_COLS)[:, SUBLANES:, :]
    sprev = sprev.reshape(rows, SHIFT_COLS)
    hist_sc[...] = s3[:, c - SUBLANES:, :]
    s = s3.reshape(rows, SHIFT_COLS)
    xs = s + (sprev - s) * mu_ref[...]
    r = xs[:, 0:hw]
    k = xs[:, hw:2 * hw]
    v = xs[:, 2 * hw:3 * hw]
    la = xs[:, 3 * hw:3 * hw + LANES]
    lane = lax.broadcasted_iota(jnp.int32, (rows, LANES), 1)
    lin = jnp.where(lane < RW_HD, jnp.tanh(la), la).astype(BF16)
    lo = _dot(lin, lora_ref[...])
    wlog = -_softplus(-(w0_ref[...] + lo[:, :hw])) - 0.5
    logw = -jnp.exp(wlog)
    a = _sigmoid(a0_ref[...] + lo[:, hw:])
    bones = bones_ref[...]
    kk = k * kkp_ref[...]
    kk = kk * lax.rsqrt(_dot_sel_r(kk * kk, bones) + 1e-12)
    kf = k * (1.0 + (a - 1.0) * ka_ref[...])
    b = kk * a

    ri = lax.broadcasted_iota(jnp.int32, (rows, rows), 0)
    cj = lax.broadcasted_iota(jnp.int32, (rows, rows), 1)
    same_seq = _div_pow2(ri, c) == _div_pow2(cj, c)
    ltri = jnp.where(same_seq & (cj <= ri), 1.0, 0.0).astype(BF16)
    lseq = jnp.where(same_seq, 1.0, 0.0).astype(BF16)
    log_g = _dot_sel_l3(ltri, logw)
    log_end = _dot_sel_l3(lseq, logw)
    g_inv = jnp.exp(-log_g)
    g_rem = jnp.exp(log_end - log_g)
    g_end = jnp.exp(log_end)

    low = lax.broadcasted_iota(jnp.int32, (1, 1, LANES), 2) < RW_HD

    def to_tb(x):
        return jnp.stack([x[tb * kb:tb * (kb + 1), LANES * p:LANES * (p + 1)]
                          for p in range(RW_PAIRS) for kb in range(ntb)], axis=0)

    def stack2(x3):
        return jnp.concatenate([jnp.where(low, x3, 0.0), jnp.where(low, 0.0, x3)], axis=1)

    def dup2(x3):
        return jnp.concatenate([x3, x3], axis=1)

    def to_ps(x3):
        if spt == 1:
            return x3
        return jnp.stack([jnp.concatenate([x3[g, c * q:c * (q + 1)], x3[g, tb + c * q:tb + c * (q + 1)]], axis=0)
                          for g in range(gt) for q in range(spt)], axis=0)

    def from_ps(y3):
        if spt == 1:
            return y3
        return jnp.stack([jnp.concatenate([y3[g * spt + q, :c] for q in range(spt)]
                                          + [y3[g * spt + q, c:] for q in range(spt)], axis=0)
                          for g in range(gt)], axis=0)

    xk = stack2(to_tb(kk * jnp.exp(log_g - logw)))
    xr = stack2(to_tb(r * jnp.exp(log_g)))
    vst = stack2(to_tb(v))
    kgs = stack2(to_tb(kf * g_rem))
    bgs = stack2(to_tb(b * g_rem))
    yb = dup2(to_tb(b * g_inv)).astype(BF16)
    yk = dup2(to_tb(kf * g_inv)).astype(BF16)
    xx = jnp.concatenate([xk, xr], axis=1).astype(BF16)
    ab = _bdot_nt(xx, yb)
    ak = _bdot_nt(xx, yk)

    n2 = 2 * tb
    i2 = lax.broadcasted_iota(jnp.int32, (1, n2, n2), 1)
    j2 = lax.broadcasted_iota(jnp.int32, (1, n2, n2), 2)
    same_blk = _div_pow2(i2, c) == _div_pow2(j2, c)
    m_strict = same_blk & (j2 < i2)
    m_incl = same_blk & (j2 <= i2)
    eye = jnp.where(i2 == j2, 1.0, 0.0)
    a_bk = jnp.where(m_strict, ab[:, :n2], 0.0)
    a_br = jnp.where(m_incl, ab[:, n2:], 0.0)
    a_kk = jnp.where(m_strict, ak[:, :n2], 0.0)
    a_kr = jnp.where(m_incl, ak[:, n2:], 0.0)

    pw = -a_bk
    tinv = eye + pw
    for _ in range(int(math.log2(c)) - 1):
        pw = _bdot16(pw, pw)
        tinv = tinv + _bdot16(tinv, pw)

    st = s_sc[...].reshape(gs, LANES, LANES)
    stb = st.astype(BF16)
    if spt == 1:
        res = _bdot_nt(xx, stb)
        xks, xrs = res[:, :n2], res[:, n2:]
    else:
        xks = from_ps(_bdot_nt(to_ps(xk).astype(BF16), stb))
        xrs = from_ps(_bdot_nt(to_ps(xr).astype(BF16), stb))
    vb = vst.astype(BF16)
    uu = _bdot16(tinv, xks + _bdot(a_kk.astype(BF16), vb))
    ost = xrs + _bdot(a_kr.astype(BF16), vb) - _bdot(a_br.astype(BF16), uu.astype(BF16))
    o3 = ost[:, :tb] + ost[:, tb:]

    lhs = jnp.concatenate([to_ps(vst), -to_ps(uu)], axis=1).astype(BF16)
    rhs = jnp.concatenate([to_ps(kgs), to_ps(bgs)], axis=1).astype(BF16)
    ds_ = _bdot_tn(lhs, rhs)
    gend = jnp.stack([g_end[c * q:c * q + 1, LANES * p:LANES * (p + 1)]
                      for p in range(RW_PAIRS) for q in range(sb)], axis=0)
    s_sc[...] = (st * gend + ds_).reshape(RW_PAIRS, sb, LANES, LANES)

    o = jnp.concatenate([jnp.concatenate([o3[p * ntb + kb] for kb in range(ntb)], axis=0)
                         for p in range(RW_PAIRS)], axis=1)
    inv_hd = 1.0 / RW_HD
    mean = _dot_sel_r(o, bones) * inv_hd
    d = o - mean
    var = _dot_sel_r(d * d, bones) * inv_hd
    on = d * lax.rsqrt(var + GN_EPS) * gng_ref[...] + gnb_ref[...]
    bonus = _dot_sel_r(r * kf * rk_ref[...], bones) * v
    g = bg_ref[...].reshape(rows, hw)
    ob_ref[...] = ((on + bonus) * (g * _sigmoid(g))).astype(BF16).reshape(sb, c, hw)

    @pl.when(ci == pl.num_programs(1) - 1)
    def _():
        sout_ref[...] = s_sc[...]


def _rwkv(bs, bg, pinit, s0, p, *, sb, c):
    n_seq, t, _ = bs.shape
    hw = RW_WIDTH
    assert (sb * c) % RW_TB == 0 and RW_TB % c == 0
    kern = functools.partial(_rwkv_kernel, sb=sb, c=c)
    blk = lambda s: pl.BlockSpec(s, lambda i, j: (i, j, 0))
    st_spec = pl.BlockSpec((RW_PAIRS, sb, LANES, LANES), lambda i, j: (0, i, 0, 0))
    return pl.pallas_call(
        kern,
        grid=(n_seq // sb, t // c),
        in_specs=[
            blk((sb, c, SHIFT_COLS)), blk((sb, c, hw)),
            pl.BlockSpec((sb, SUBLANES, SHIFT_COLS), lambda i, j: (i, 0, 0)),
            st_spec,
            _const_spec((1, SHIFT_COLS)), _const_spec((1, hw)), _const_spec((LANES, 2 * hw)),
            _const_spec((1, hw)), _const_spec((1, hw)), _const_spec((1, hw)), _const_spec((1, hw)),
            _const_spec((1, hw)), _const_spec((1, hw)), _const_spec((hw, hw)),
        ],
        out_specs=[blk((sb, c, hw)), st_spec],
        out_shape=[jax.ShapeDtypeStruct((n_seq, t, hw), BF16),
                   jax.ShapeDtypeStruct((RW_PAIRS, n_seq, LANES, LANES), F32)],
        scratch_shapes=[pltpu.VMEM((sb, SUBLANES, SHIFT_COLS), F32),
                        pltpu.VMEM((RW_PAIRS, sb, LANES, LANES), F32)],
        compiler_params=_cparams(("arbitrary", "arbitrary")),
        name="rwkv7",
    )(bs, bg, pinit, s0, p["rw_mu"], p["rw_w0"], p["rw_lora"], p["rw_a0"], p["rw_kk"], p["rw_ka"],
      p["rw_rk"], p["rw_gn_g"], p["rw_gn_b"], p["bones"])


def _rope128(x, cos, sin_a, sin_b):
    return x * cos + pltpu.roll(x, LANES - ROPE_DIM // 2, 1) * sin_a + pltpu.roll(x, ROPE_DIM // 2, 1) * sin_b


def _mla_prep_kernel(cq_ref, ckv_ref, pe_ref, cos_ref, sa_ref, sb_ref, gqa_ref, wq_ref, gq_ref,
                     gkva_ref, wk_ref, gk_ref, wvt_ref, ckvn_ref, q_ref, *kv_refs, sb, tt):
    rows = sb * tt
    cq = cq_ref[...].reshape(rows, Q_RANK)
    cqn = (cq * lax.rsqrt(jnp.mean(cq * cq, axis=-1, keepdims=True) + EPS) * gqa_ref[...]).astype(BF16)
    q = _dot(cqn, wq_ref[...])
    ckv = ckv_ref[...].reshape(rows, KV_RANK)
    ckvn = ckv * lax.rsqrt(jnp.mean(ckv * ckv, axis=-1, keepdims=True) + EPS) * gkva_ref[...]
    ckvn_ref[...] = ckvn.reshape(sb, tt, KV_RANK)

    def tile(ref):
        return jnp.broadcast_to(ref[...][None], (sb, tt, LANES)).reshape(rows, LANES)

    cos, sa, sb_ = tile(cos_ref), tile(sa_ref), tile(sb_ref)
    inv_d = 1.0 / QK_DIM
    for h in range(MLA_HEADS):
        qh = q[:, h * LANES:(h + 1) * LANES]
        qn = qh * lax.rsqrt(jnp.sum(qh * qh, axis=-1, keepdims=True) * inv_d + EPS) * gq_ref[...]
        q_ref[:, h] = (_rope128(qn, cos, sa, sb_) * SCALE).astype(BF16).reshape(sb, tt, LANES)
    if not kv_refs:
        return
    k_ref, vt_ref = kv_refs
    cb = ckvn.astype(BF16)
    kvk = _dot(cb, wk_ref[...])
    vt = _dot_nt(wvt_ref[...], cb)
    pe = pe_ref[...].reshape(rows, LANES)
    for h in range(MLA_HEADS):
        kh = kvk[:, h * LANES:(h + 1) * LANES] + pe
        kn = kh * lax.rsqrt(jnp.sum(kh * kh, axis=-1, keepdims=True) * inv_d + EPS) * gk_ref[...]
        k_ref[:, h] = _rope128(kn, cos, sa, sb_).astype(BF16).reshape(sb, tt, LANES)
        vt_ref[0, h] = vt[h * V_DIM:(h + 1) * V_DIM].astype(BF16)


def _mla_prep(cq, ckv, pe_a, tabs, p, *, sb, tt, with_kv):
    n_seq, t, _ = cq.shape
    assert sb == 1 or not with_kv
    kern = functools.partial(_mla_prep_kernel, sb=sb, tt=tt)
    blk = lambda w: pl.BlockSpec((sb, tt, w), lambda i, j: (i, j, 0))
    tab = pl.BlockSpec((tt, LANES), lambda i, j: (j, 0))
    hd = pl.BlockSpec((sb, MLA_HEADS, tt, LANES), lambda i, j: (i, 0, j, 0))
    hshape = jax.ShapeDtypeStruct((n_seq, MLA_HEADS, t, LANES), BF16)
    out_specs = [blk(KV_RANK), hd]
    out_shape = [jax.ShapeDtypeStruct((n_seq, t, KV_RANK), F32), hshape]
    if with_kv:
        out_specs += [hd, pl.BlockSpec((sb, MLA_HEADS, V_DIM, tt), lambda i, j: (i, 0, 0, j))]
        out_shape += [hshape, jax.ShapeDtypeStruct((n_seq, MLA_HEADS, V_DIM, t), BF16)]
    return pl.pallas_call(
        kern,
        grid=(n_seq // sb, t // tt),
        in_specs=[
            blk(Q_RANK), blk(KV_RANK), blk(LANES), tab, tab, tab,
            _const_spec((1, Q_RANK)), _const_spec((Q_RANK, MLA_HEADS * LANES)), _const_spec((1, LANES)),
            _const_spec((1, KV_RANK)), _const_spec((KV_RANK, MLA_HEADS * LANES)), _const_spec((1, LANES)),
            _const_spec((MLA_WIDTH, KV_RANK)),
        ],
        out_specs=out_specs,
        out_shape=out_shape,
        compiler_params=_cparams(("arbitrary", "arbitrary")),
        name="mla_prep",
    )(cq, ckv, pe_a, tabs[0], tabs[1], tabs[2], p["g_qa"], p["wq"], p["gq128"], p["g_kva"],
      p["wk"], p["gk128"], p["wvt"])


def _flash_kernel(q_ref, k_ref, vt_ref, o_ref, *, tq, tk):
    qi = pl.program_id(2)
    nsub = tq // tk
    qs = [q_ref[0, hh] for hh in range(2)]

    def tile(hh, j, carry, mask):
        m, l, acc = carry
        kj = k_ref[0, hh, pl.ds(j * tk, tk), :]
        vj = vt_ref[0, hh, :, pl.ds(j * tk, tk)]
        s = _dot_nt(kj, qs[hh])
        if mask is not None:
            s = jnp.where(mask, s, NEG)
        m_new = jnp.maximum(m, jnp.max(s, axis=0, keepdims=True))
        pr = jnp.exp(s - m_new)
        corr = jnp.exp(m - m_new)
        l = l * corr + jnp.sum(pr, axis=0, keepdims=True)
        acc = acc * corr + _dot(vj, pr.astype(BF16))
        return m_new, l, acc

    def step(j, carry):
        return tuple(tile(hh, j, carry[hh], None) for hh in range(2))

    init1 = (jnp.full((1, tq), NEG, F32), jnp.zeros((1, tq), F32), jnp.zeros((V_DIM, tq), F32))
    carry = lax.fori_loop(0, qi * nsub, step, (init1, init1))
    krow = lax.broadcasted_iota(jnp.int32, (tk, tq), 0)
    qcol = lax.broadcasted_iota(jnp.int32, (tk, tq), 1)
    for d in range(nsub):
        mask = (krow + d * tk) <= qcol
        carry = tuple(tile(hh, qi * nsub + d, carry[hh], mask) for hh in range(2))
    ot = jnp.concatenate([carry[hh][2] / carry[hh][1] for hh in range(2)], axis=0)
    o_ref[0] = ot.T.astype(BF16)


def _flash(q, k, vt, *, tq, tk):
    b, h, t, _ = q.shape
    kern = functools.partial(_flash_kernel, tq=tq, tk=tk)
    return pl.pallas_call(
        kern,
        grid=(b, h // 2, t // tq),
        in_specs=[
            pl.BlockSpec((1, 2, tq, LANES), lambda bi, pi, qi: (bi, pi, qi, 0)),
            pl.BlockSpec((1, 2, t, LANES), lambda bi, pi, qi: (bi, pi, 0, 0)),
            pl.BlockSpec((1, 2, V_DIM, t), lambda bi, pi, qi: (bi, pi, 0, 0)),
        ],
        out_specs=pl.BlockSpec((1, tq, LANES), lambda bi, pi, qi: (bi, qi, pi)),
        out_shape=jax.ShapeDtypeStruct((b, t, (h // 2) * LANES), BF16),
        compiler_params=_cparams(("arbitrary", "arbitrary", "arbitrary")),
        name="mla_flash",
    )(q, k, vt)


def _decode_kernel(pt_ref, q_ref, cnew_ref, penew_ref, ckv_hbm, kpe_hbm, wukt_ref, gkn_ref, sel_ref,
                   wukp_ref, e16_ref, gkr_ref, gkrt_ref, ct_ref, st_ref, tn_ref, wuv_ref, o_ref,
                   cbuf, pbuf, s_sc, cb_sc, sem, *, layer, pc, n_chunks, ts):
    b = pl.program_id(0)
    nb = pl.num_programs(0)
    page = ckv_hbm.shape[2]
    hq = MLA_HEADS * ts
    half = ROPE_DIM // 2

    def copies(bb, ch, slot):
        out = []
        for i in range(pc):
            phys = pt_ref[bb, ch * pc + i]
            out.append(pltpu.make_async_copy(
                ckv_hbm.at[layer, phys], cbuf.at[slot, pl.ds(i * page, page), :], sem.at[0, slot]))
            out.append(pltpu.make_async_copy(
                kpe_hbm.at[layer, phys], pbuf.at[slot, :, pl.ds(i * page, page)], sem.at[1, slot]))
        return out

    @pl.when(b == 0)
    def _():
        for cp in copies(0, 0, 0):
            cp.start()

    qf = q_ref[0].astype(F32).reshape(hq, LANES)
    qg = qf * gkn_ref[...]
    qlat = jnp.concatenate(
        [_dot(qg[h * ts:(h + 1) * ts].astype(BF16), wukt_ref[h]) for h in range(MLA_HEADS)],
        axis=0).astype(BF16)
    qrope = _dot(qf.astype(BF16), sel_ref[...])
    qrope32 = qrope[:, :ROPE_DIM].astype(BF16)
    qrope128 = qrope.astype(BF16)
    e16 = e16_ref[...]
    inv_d = 1.0 / QK_DIM
    n_col = MLA_HEADS * NOPE_DIM // LANES

    def nope_ssq(cb):
        kn = _dot(cb, wukp_ref[...])
        part = kn[:, 0:LANES] * kn[:, 0:LANES]
        for cc in range(1, n_col):
            kc = kn[:, cc * LANES:(cc + 1) * LANES]
            part = part + kc * kc
        return _dot_nt(e16, part.astype(BF16))

    def update(carry, s, cb):
        m, l, acc = carry
        m_new = jnp.maximum(m, jnp.max(s, axis=-1, keepdims=True))
        pr = jnp.exp(s - m_new)
        corr = jnp.exp(m - m_new)
        l = l * corr + jnp.sum(pr, axis=-1, keepdims=True)
        acc = acc * corr + _dot(pr.astype(BF16), cb)
        return m_new, l, acc

    def dma_step(ch):
        slot = (b * n_chunks + ch) % 2
        for cp in copies(b, ch, slot):
            cp.wait()

        @pl.when(ch + 1 < n_chunks)
        def _():
            for cp in copies(b, ch + 1, 1 - slot):
                cp.start()

        @pl.when((ch + 1 == n_chunks) & (b + 1 < nb))
        def _():
            for cp in copies(b + 1, 0, 1 - slot):
                cp.start()
        return slot

    def stage_a(ch, slot, par):
        krs, pss = [], []
        for i in range(pc):
            pg = ch * pc + i
            x = pbuf[slot, :, i * page:(i + 1) * page]
            xg = x * gkrt_ref[...]
            sw = jnp.concatenate([xg[half:], xg[:half]], axis=0)
            krs.append((xg * ct_ref[pg] + sw * st_ref[pg]).astype(BF16))
            pss.append(jnp.sum(x * x, axis=0, keepdims=True))
        cb = cbuf[slot].astype(BF16)
        kr = jnp.concatenate(krs, axis=1)
        pesq = jnp.concatenate(pss, axis=1)
        ssq = nope_ssq(cb) + pesq
        s = _dot_nt(qlat, cb) + _dot(qrope32, kr)
        s_sc[par] = s * lax.rsqrt(ssq * inv_d + EPS)
        cb_sc[par] = cb

    def step(ch, par, carry):
        slot = dma_step(ch)
        stage_a(ch, slot, par)
        return update(carry, s_sc[1 - par], cb_sc[1 - par])

    init = (jnp.full((hq, 1), NEG, F32), jnp.zeros((hq, 1), F32), jnp.zeros((hq, LANES), F32))
    stage_a(0, dma_step(0), 0)
    n_steps = n_chunks - 1

    def pair(j, carry):
        carry = step(2 * j + 1, 1, carry)
        return step(2 * j + 2, 0, carry)

    carry = lax.fori_loop(0, n_steps // 2, pair, init)
    if n_steps % 2:
        carry = step(n_chunks - 1, 1, carry)
    last = (n_chunks - 1) % 2
    carry = update(carry, s_sc[last], cb_sc[last])

    pad = jnp.zeros((NEW_ROWS - ts, LANES), F32)
    cn = jnp.concatenate([cnew_ref[0], pad], axis=0)
    pn = jnp.concatenate([penew_ref[0], pad], axis=0)
    krn = _rope128(pn * gkr_ref[...], tn_ref[0], tn_ref[1], tn_ref[2]).astype(BF16)
    cnb = cn.astype(BF16)
    ones_r = jnp.ones((hq, LANES), BF16)
    ssq = nope_ssq(cnb) + _dot_nt(ones_r, (pn * pn).astype(BF16))
    s = (_dot_nt(qlat, cnb) + _dot_nt(qrope128, krn)) * lax.rsqrt(ssq * inv_d + EPS)
    qidx = _mod_pow2(lax.broadcasted_iota(jnp.int32, (hq, NEW_ROWS), 0), ts)
    tidx = lax.broadcasted_iota(jnp.int32, (hq, NEW_ROWS), 1)
    s = jnp.where(tidx <= qidx, s, NEG)
    _, l, acc = update(carry, s, cnb)
    al = acc / l
    out = _dot(al[0:ts].astype(BF16), wuv_ref[0])
    for h in range(1, MLA_HEADS):
        out = out + _dot(al[h * ts:(h + 1) * ts].astype(BF16), wuv_ref[h])
    o_ref[0] = out


def _decode(page_table, q, cnew, penew, cache_ckv, cache_kpe_t, tabs_c, tabs_s, tabs_n, p, *, layer, pc):
    db, h, ts, _ = q.shape
    n_pages = page_table.shape[1]
    page = cache_ckv.shape[2]
    assert n_pages % pc == 0
    n_chunks = n_pages // pc
    tok = pc * page
    kern = functools.partial(_decode_kernel, layer=layer, pc=pc, n_chunks=n_chunks, ts=ts)
    cs = lambda shape: pl.BlockSpec(shape, lambda b, pt: (0,) * len(shape))
    cs1 = lambda shape: pl.BlockSpec(shape, lambda b, pt: (0,) * len(shape), pipeline_mode=pl.Buffered(1))
    grid_spec = pltpu.PrefetchScalarGridSpec(
        num_scalar_prefetch=1,
        grid=(db,),
        in_specs=[
            pl.BlockSpec((1, h, ts, LANES), lambda b, pt: (b, 0, 0, 0)),
            pl.BlockSpec((1, ts, LANES), lambda b, pt: (b, 0, 0)),
            pl.BlockSpec((1, ts, LANES), lambda b, pt: (b, 0, 0)),
            pl.BlockSpec(memory_space=pl.ANY),
            pl.BlockSpec(memory_space=pl.ANY),
            cs((h, LANES, LANES)), cs((1, LANES)), cs((LANES, LANES)),
            cs((KV_RANK, h * NOPE_DIM)), cs((h * ts, LANES)), cs((1, LANES)), cs((ROPE_DIM, LANES)),
            cs1((n_pages, ROPE_DIM, page)), cs1((n_pages, ROPE_DIM, page)), cs((3, NEW_ROWS, LANES)),
            cs((h, KV_RANK, MLA_WIDTH)),
        ],
        out_specs=pl.BlockSpec((1, ts, MLA_WIDTH), lambda b, pt: (b, 0, 0)),
        scratch_shapes=[
            pltpu.VMEM((2, tok, KV_RANK), F32),
            pltpu.VMEM((2, ROPE_DIM, tok), F32),
            pltpu.VMEM((2, h * ts, tok), F32),
            pltpu.VMEM((2, tok, KV_RANK), BF16),
            pltpu.SemaphoreType.DMA((2, 2)),
        ],
    )
    return pl.pallas_call(
        kern,
        grid_spec=grid_spec,
        out_shape=jax.ShapeDtypeStruct((db, ts, MLA_WIDTH), F32),
        compiler_params=_cparams(("arbitrary",)),
        name="mla_decode",
    )(page_table, q, cnew, penew, cache_ckv, cache_kpe_t, p["wukt"], p["gkn128"], p["sel_rope"],
      p["wukp"], p["e16"], p["gkr128"], p["gkrt"], tabs_c, tabs_s, tabs_n, p["wuv"])


def _merge_kernel(x_ref, gate_ref, oa_ref, ob_ref, oc_ref, cg_ref, woa_ref, wob_ref, woc_ref,
                  wout_ref, y_ref):
    d = x_ref.shape[1]
    cg = cg_ref[...]
    oc = (oc_ref[...].astype(F32) * (cg * _sigmoid(cg))).astype(BF16)
    g = _sigmoid(gate_ref[...])
    merged = (g[:, 0:d] * _dot(oa_ref[...], woa_ref[...])
              + g[:, d:2 * d] * _dot(ob_ref[...], wob_ref[...])
              + g[:, 2 * d:3 * d] * _dot(oc, woc_ref[...]))
    y_ref[...] = x_ref[...] + _dot(merged.astype(BF16), wout_ref[...])


def _merge(x2d, gate, oa, ob, oc, cg, p):
    n, d = x2d.shape
    tm = min(512, n)
    row = lambda w: pl.BlockSpec((tm, w), lambda i: (i, 0))
    return pl.pallas_call(
        _merge_kernel,
        grid=(n // tm,),
        in_specs=[row(d), row(3 * d), row(LRU_WIDTH), row(RW_WIDTH), row(MLA_WIDTH), row(MLA_WIDTH),
                  _const_spec((LRU_WIDTH, d)), _const_spec((RW_WIDTH, d)), _const_spec((MLA_WIDTH, d)),
                  _const_spec((d, d))],
        out_specs=row(d),
        out_shape=jax.ShapeDtypeStruct((n, d), F32),
        compiler_params=_cparams(("arbitrary",)),
        name="merge",
    )(x2d, gate, oa, ob, oc, cg, p["w_oa"], p["w_ob"], p["w_oc"], p["w_out"])


def _rope_angles(pos):
    inv = 1.0 / (ROPE_THETA ** (jnp.arange(0, ROPE_DIM, 2, dtype=F32) / ROPE_DIM))
    ang = pos.astype(F32)[:, None] * inv[None, :]
    return jnp.cos(ang), jnp.sin(ang)


def _rope_pattern(pos, lane0, rows=None):
    half = ROPE_DIM // 2
    cos, sin = _rope_angles(pos)
    n = pos.shape[0]
    ones_before = jnp.ones((n, lane0), F32) if lane0 else jnp.zeros((n, 0), F32)
    z = lambda w: jnp.zeros((n, w), F32)
    rest = LANES - lane0 - ROPE_DIM
    cos_t = jnp.concatenate([ones_before, cos, cos, z(rest)], axis=1)
    sin_a = jnp.concatenate([z(lane0), -sin, z(half), z(rest)], axis=1)
    sin_b = jnp.concatenate([z(lane0), z(half), sin, z(rest)], axis=1)
    tabs = jnp.stack([cos_t, sin_a, sin_b], axis=0)
    if rows is not None and rows > n:
        tabs = jnp.pad(tabs, ((0, 0), (0, rows - n), (0, 0)))
    return tabs


def _rope_pages(n_pages, page):
    cos, sin = _rope_angles(jnp.arange(n_pages * page))
    ct = jnp.concatenate([cos, cos], axis=1)
    st = jnp.concatenate([-sin, sin], axis=1)
    lay = lambda x: jnp.transpose(x.reshape(n_pages, page, ROPE_DIM), (0, 2, 1))
    return lay(ct), lay(st)


def _block_diag2(w):
    nb, bs, _ = w.shape
    per = nb // 2
    out = jnp.zeros((2, per * bs, per * bs), w.dtype)
    for i in range(nb):
        hf, j = divmod(i, per)
        out = out.at[hf, j * bs:(j + 1) * bs, j * bs:(j + 1) * bs].set(w[i])
    return out


def _layer_params(l, ts, a):
    d = a["w_in"].shape[1]
    w = a["w_in"][l]
    src_pe = SEG_CKV[1]
    z = lambda n: jnp.zeros((d, n), w.dtype)
    w_pad = jnp.concatenate([
        w[:, :src_pe],
        z(64), w[:, src_pe:src_pe + ROPE_DIM], z(32),
        w[:, src_pe:src_pe + ROPE_DIM], z(96),
        w[:, src_pe + ROPE_DIM:],
    ], axis=1).astype(BF16)
    assert w_pad.shape[1] == IN_COLS_PAD
    row = lambda v: v.reshape(1, -1)
    hd = RW_HD
    bones = jnp.kron(jnp.eye(RW_WIDTH // hd, dtype=F32), jnp.ones((hd, hd), F32)).astype(BF16)
    zl = jnp.zeros((LORA, RW_WIDTH), F32)
    lora = jnp.concatenate([jnp.concatenate([a["rw_w2"][l], zl], axis=1),
                            jnp.concatenate([zl, a["rw_a2"][l]], axis=1)], axis=0).astype(BF16)
    w_uq = a["mla_w_uq"][l]
    w_ukv = a["mla_w_ukv"][l]
    w_uk, w_uv = w_ukv[..., :NOPE_DIM], w_ukv[..., NOPE_DIM:]
    g_kn = a["mla_g_kn"][l]
    pad_h = lambda x: jnp.pad(x, ((0, 0), (0, 0), (0, LANES - x.shape[-1]))).reshape(x.shape[0], -1)
    wukt = jnp.pad(jnp.transpose(w_uk, (1, 2, 0)), ((0, 0), (0, LANES - NOPE_DIM), (0, 0)))
    wukp = jnp.transpose(w_uk.reshape(KV_RANK, MLA_HEADS, NOPE_DIM // 16, 16), (0, 2, 1, 3))
    wukp = wukp.reshape(KV_RANK, MLA_HEADS * NOPE_DIM)
    e16 = jnp.repeat(jnp.kron(jnp.eye(MLA_HEADS, dtype=F32), jnp.ones((1, 16), F32)), ts, axis=0)
    wuv = jnp.zeros((MLA_HEADS, KV_RANK, MLA_WIDTH), F32)
    for h in range(MLA_HEADS):
        wuv = wuv.at[h, :, h * V_DIM:(h + 1) * V_DIM].set(w_uv[:, h, :])
    sel = jnp.zeros((LANES, LANES), F32).at[
        jnp.arange(NOPE_DIM, QK_DIM), jnp.arange(ROPE_DIM)].set(1.0)
    return {
        "norm_g": row(a["norm_g"][l]), "w_in": w_pad,
        "conv_w8": jnp.pad(a["conv_w"][l], ((0, SUBLANES - a["conv_w"].shape[1]), (0, 0))),
        "conv_b": row(a["conv_b"][l]),
        "lru_wr2": _block_diag2(a["lru_wr"][l]).astype(BF16), "lru_br": row(a["lru_br"][l]),
        "lru_wi2": _block_diag2(a["lru_wi"][l]).astype(BF16), "lru_bi": row(a["lru_bi"][l]),
        "lru_lam": row(a["lru_lam"][l]),
        "rw_mu": row(a["rw_mu"][l]), "rw_w0": row(a["rw_w0"][l]), "rw_lora": lora,
        "rw_a0": row(a["rw_a0"][l]), "rw_kk": row(a["rw_kk"][l]), "rw_ka": row(a["rw_ka"][l]),
        "rw_rk": row(a["rw_rk"][l]), "rw_gn_g": row(a["rw_gn_g"][l]), "rw_gn_b": row(a["rw_gn_b"][l]),
        "bones": bones,
        "g_qa": row(a["mla_g_qa"][l]), "wq": pad_h(w_uq).astype(BF16),
        "gq128": row(jnp.pad(a["mla_g_qn"][l], (0, LANES - QK_DIM))),
        "g_kva": row(a["mla_g_kva"][l]), "wk": pad_h(w_uk).astype(BF16),
        "gk128": row(jnp.pad(g_kn, (0, LANES - QK_DIM))),
        "wvt": jnp.transpose(w_uv.reshape(KV_RANK, MLA_WIDTH)).astype(BF16),
        "wukt": wukt.astype(BF16),
        "gkn128": row(jnp.pad(g_kn[:NOPE_DIM], (0, LANES - NOPE_DIM))),
        "sel_rope": sel.astype(BF16), "wukp": wukp.astype(BF16), "e16": e16.astype(BF16),
        "gkr128": row(jnp.pad(g_kn[NOPE_DIM:], (0, LANES - ROPE_DIM))),
        "gkrt": jnp.broadcast_to(g_kn[NOPE_DIM:, None], (ROPE_DIM, LANES)),
        "wuv": wuv.astype(BF16),
        "w_oa": a["w_oa"][l].astype(BF16), "w_ob": a["w_ob"][l].astype(BF16),
        "w_oc": a["w_oc"][l].astype(BF16), "w_out": a["w_out"][l].astype(BF16),
    }


def _state_to_pairs(s):
    n = s.shape[0]
    s = jnp.transpose(s.reshape(n, RW_PAIRS, 2, RW_HD, RW_HD), (1, 0, 2, 3, 4))
    z = jnp.zeros_like(s[:, :, 0])
    top = jnp.concatenate([s[:, :, 0], z], axis=-1)
    bot = jnp.concatenate([z, s[:, :, 1]], axis=-1)
    return jnp.concatenate([top, bot], axis=-2)


def _pairs_to_state(sp):
    n = sp.shape[1]
    a = sp[:, :, :RW_HD, :RW_HD]
    b = sp[:, :, RW_HD:, RW_HD:]
    return jnp.transpose(jnp.stack([a, b], axis=2), (1, 0, 2, 3, 4)).reshape(n, 2 * RW_PAIRS, RW_HD, RW_HD)


def _hist_rows(x):
    return jnp.pad(x, ((0, 0), (SUBLANES - x.shape[1], 0), (0, 0)))


def _mixer_layer(x, p, lru_h0, conv_buf, rw_s0, rw_prev, tabs, attend, *, lru_blk, rw_blk, prep_blk,
                 with_kv):
    n_seq, t, d = x.shape
    n = n_seq * t
    segs = _inproj(x.reshape(n, d), p["norm_g"], p["w_in"])
    ax, ag, bs, bg, cq, ckv_raw, pe_a, pe_b, cg, gate = segs
    r3 = lambda v: v.reshape(n_seq, t, v.shape[-1])
    oa, lru_h = _lru(r3(ax), r3(ag), _hist_rows(conv_buf), lru_h0[:, None, :], p, sb=lru_blk[0],
                     c=lru_blk[1])
    lru_h = lru_h[:, 0, :]
    if rw_s0 is None:
        s0 = jnp.zeros((RW_PAIRS, n_seq, LANES, LANES), x.dtype)
    else:
        s0 = _state_to_pairs(rw_s0)
    ob, s_pairs = _rwkv(r3(bs), r3(bg), _hist_rows(rw_prev[:, None, :]), s0, p, sb=rw_blk[0], c=rw_blk[1])
    prep = _mla_prep(r3(cq), r3(ckv_raw), r3(pe_a), tabs, p, sb=prep_blk[0], tt=prep_blk[1],
                     with_kv=with_kv)
    ckv = prep[0]
    oc = attend(prep, r3(pe_b))
    y = _merge(x.reshape(n, d), gate, oa.reshape(n, -1), ob.reshape(n, -1), oc.reshape(n, -1), cg, p)
    conv_new = r3(ax)[:, t - conv_buf.shape[1]:, :]
    shift_new = r3(bs)[:, t - 1, :]
    kpe = r3(pe_b)[:, :, :ROPE_DIM]
    return y.reshape(n_seq, t, d), (lru_h, conv_new, _pairs_to_state(s_pairs), shift_new, ckv, kpe)


def kernel(x_prompt, x_sample, state_lru_h, state_lru_conv, state_rwkv_S, state_rwkv_shift, cache_ckv, cache_kpe, page_table, norm_g, w_in, conv_w, conv_b, lru_wr, lru_br, lru_wi, lru_bi, lru_lam, rw_mu, rw_w0, rw_w2, rw_a0, rw_a2, rw_kk, rw_ka, rw_rk, rw_gn_g, rw_gn_b, mla_g_qa, mla_g_kva, mla_w_uq, mla_w_ukv, mla_g_qn, mla_g_kn, w_oa, w_ob, w_oc, w_out):
    a = dict(norm_g=norm_g, w_in=w_in, conv_w=conv_w, conv_b=conv_b, lru_wr=lru_wr, lru_br=lru_br,
             lru_wi=lru_wi, lru_bi=lru_bi, lru_lam=lru_lam, rw_mu=rw_mu, rw_w0=rw_w0, rw_w2=rw_w2,
             rw_a0=rw_a0, rw_a2=rw_a2, rw_kk=rw_kk, rw_ka=rw_ka, rw_rk=rw_rk, rw_gn_g=rw_gn_g,
             rw_gn_b=rw_gn_b, mla_g_qa=mla_g_qa, mla_g_kva=mla_g_kva, mla_w_uq=mla_w_uq,
             mla_w_ukv=mla_w_ukv, mla_g_qn=mla_g_qn, mla_g_kn=mla_g_kn, w_oa=w_oa, w_ob=w_ob,
             w_oc=w_oc, w_out=w_out)
    bsz, t, _ = x_prompt.shape
    db, ts, _ = x_sample.shape
    depth = w_in.shape[0]
    n_pages = page_table.shape[1]
    page = cache_ckv.shape[2]
    past = n_pages * page
    dt = x_prompt.dtype
    assert ts == SUBLANES and t % RW_TB == 0 and db % SUBLANES == 0

    tabs_p = _rope_pattern(jnp.arange(t), NOPE_DIM)
    tabs_s = _rope_pattern(past + jnp.arange(ts), NOPE_DIM)
    dec_c, dec_s = _rope_pages(n_pages, page)
    dec_n = _rope_pattern(past + jnp.arange(ts), 0, rows=NEW_ROWS)
    cache_kpe_t = jnp.transpose(cache_kpe, (0, 1, 3, 2))

    tq = min(FLASH_TQ, t)
    tk = min(FLASH_TK, t)
    pc = min(DECODE_PAGES, n_pages)
    lru_c = min(256, t)
    prep_tt = min(512, t)
    sb_s = min(32, db)

    hp, hs = x_prompt, x_sample
    st_p, st_s = [], []
    for l in range(depth):
        p = _layer_params(l, ts, a)

        def attend_prompt(prep, pe_b):
            _, q, k, vt = prep
            return _flash(q, k, vt, tq=tq, tk=tk)

        def attend_sample(prep, pe_b, l=l, p=p):
            ckv, q = prep
            return _decode(page_table, q, ckv, pe_b, cache_ckv, cache_kpe_t, dec_c, dec_s, dec_n, p,
                           layer=l, pc=pc)

        hp, sp = _mixer_layer(
            hp, p, jnp.zeros((bsz, LRU_WIDTH), dt), jnp.zeros((bsz, 3, LRU_WIDTH), dt), None,
            jnp.zeros((bsz, SHIFT_COLS), dt), tabs_p, attend_prompt, lru_blk=(bsz, lru_c),
            rw_blk=(bsz, RW_TB), prep_blk=(1, prep_tt), with_kv=True)
        hs, ss = _mixer_layer(
            hs, p, state_lru_h[l], state_lru_conv[l], state_rwkv_S[l], state_rwkv_shift[l],
            tabs_s, attend_sample, lru_blk=(sb_s, ts), rw_blk=(RW_TB // ts, ts),
            prep_blk=(sb_s, ts), with_kv=False)
        st_p.append(sp)
        st_s.append(ss)

    stack = lambda states, i: jnp.stack([s[i] for s in states], axis=0)
    return (hp, hs) + tuple(stack(st_p, i) for i in range(6)) + tuple(stack(st_s, i) for i in range(6))
```

```python
import functools
import math

import jax
import jax.numpy as jnp
from jax import lax
from jax.experimental import pallas as pl
from jax.experimental.pallas import tpu as pltpu

F32 = jnp.float32
BF16 = jnp.bfloat16

EPS = 1e-6
LRU_C = 8.0
GN_EPS = 64e-5
ROPE_THETA = 10000.0
NEG = -1e30

LANES = 128
SUBLANES = 8
VMEM_LIMIT_BYTES = 56 * 1024 * 1024

LRU_WIDTH = 512
RW_WIDTH = 512
RW_HD = 64
RW_PAIRS = RW_WIDTH // LANES
LORA = 64
SHIFT_COLS = 3 * RW_WIDTH + 2 * LORA
Q_RANK = 256
KV_RANK = 128
MLA_HEADS = 8
NOPE_DIM = 64
ROPE_DIM = 32
V_DIM = 64
QK_DIM = NOPE_DIM + ROPE_DIM
MLA_WIDTH = MLA_HEADS * V_DIM
SCALE = QK_DIM ** -0.5

RW_TB = 64
FLASH_TQ = 2048
FLASH_TK = 512
DECODE_PAGES = 32
NEW_ROWS = 16
VT_ROWS = 80

SEG_AX = (0, 512)
SEG_AG = (512, 1024)
SEG_BS = (1024, 2688)
SEG_BG = (2688, 3200)
SEG_CQ = (3200, 3456)
SEG_CKV = (3456, 3584)
SEG_PEA = (3584, 3712)
SEG_PEB = (3712, 3840)
SEG_CG = (3840, 4352)
SEG_GATE = (4352, 7424)
IN_SEGS = (SEG_AX, SEG_AG, SEG_BS, SEG_BG, SEG_CQ, SEG_CKV, SEG_PEA, SEG_PEB, SEG_CG, SEG_GATE)
IN_DTYPES = (F32, BF16, F32, BF16, F32, F32, F32, F32, BF16, BF16)
IN_COLS_PAD = SEG_GATE[1]


def _cparams(sem):
    return pltpu.CompilerParams(dimension_semantics=sem, vmem_limit_bytes=VMEM_LIMIT_BYTES)


def _const_spec(shape):
    nd = len(shape)
    return pl.BlockSpec(shape, lambda *_: (0,) * nd)


def _sigmoid(x):
    return jax.nn.sigmoid(x)


def _softplus(x):
    return jnp.maximum(x, 0.0) + jnp.log(1.0 + jnp.exp(-jnp.abs(x)))


def _mod_pow2(x, n):
    return jnp.bitwise_and(x, n - 1)


def _div_pow2(x, n):
    return lax.shift_right_logical(x, int(math.log2(n)))


def _dot(a, b):
    return jnp.dot(a, b, preferred_element_type=F32)


def _dot_nt(a, b):
    return lax.dot_general(a, b, (((1,), (1,)), ((), ())), preferred_element_type=F32)


def _bdot(a, b):
    return jnp.einsum('gij,gjk->gik', a, b, preferred_element_type=F32)


def _bdot_nt(a, b):
    return jnp.einsum('gik,gjk->gij', a, b, preferred_element_type=F32)


def _bdot_tn(a, b):
    return jnp.einsum('gti,gtj->gij', a, b, preferred_element_type=F32)


def _split2(x):
    hi = x.astype(BF16)
    lo = (x - hi.astype(F32)).astype(BF16)
    return hi, lo


def _split3(x):
    hi = x.astype(BF16)
    r1 = x - hi.astype(F32)
    mid = r1.astype(BF16)
    lo = (r1 - mid.astype(F32)).astype(BF16)
    return hi, mid, lo


def _dot_sel_r(x, m01):
    hi, lo = _split2(x)
    return _dot(hi, m01) + _dot(lo, m01)


def _dot_sel_l3(m01, x):
    hi, mid, lo = _split3(x)
    return _dot(m01, hi) + _dot(m01, mid) + _dot(m01, lo)


def _bdot16(a, b):
    return _bdot(a.astype(BF16), b.astype(BF16))


def _inproj_kernel(x_ref, g_ref, w_ref, *out_refs):
    x = x_ref[...]
    ms = jnp.mean(x * x, axis=-1, keepdims=True)
    hn = (x * lax.rsqrt(ms + EPS) * g_ref[...]).astype(BF16)
    for o_ref, (lo, hi) in zip(out_refs, IN_SEGS):
        for c0 in range(lo, hi, 512):
            c1 = min(c0 + 512, hi)
            o_ref[:, c0 - lo:c1 - lo] = _dot(hn, w_ref[:, c0:c1]).astype(o_ref.dtype)


def _inproj(x2d, g, w_pad):
    n, d = x2d.shape
    tm = min(256, n)
    out_shape = [jax.ShapeDtypeStruct((n, hi - lo), dt) for (lo, hi), dt in zip(IN_SEGS, IN_DTYPES)]
    out_specs = [pl.BlockSpec((tm, hi - lo), lambda i: (i, 0)) for lo, hi in IN_SEGS]
    return pl.pallas_call(
        _inproj_kernel,
        grid=(n // tm,),
        in_specs=[
            pl.BlockSpec((tm, d), lambda i: (i, 0)),
            _const_spec((1, d)),
            pl.BlockSpec((d, IN_COLS_PAD), lambda i: (0, 0), pipeline_mode=pl.Buffered(1)),
        ],
        out_specs=out_specs,
        out_shape=out_shape,
        compiler_params=_cparams(("arbitrary",)),
        name="inproj",
    )(x2d, g, w_pad)


def _lru_kernel(ax_ref, ag_ref, cinit_ref, h0_ref, cw_ref, cb_ref, wr_ref, br_ref, wi_ref, bi_ref,
                lam_ref, oa_ref, hlast_ref, hist_sc, h_sc, *, sb, c):
    ci = pl.program_id(1)

    @pl.when(ci == 0)
    def _():
        hist_sc[...] = cinit_ref[...]
        h_sc[...] = h0_ref[...]

    w = LRU_WIDTH
    u = ax_ref[...]
    ext = jnp.concatenate([hist_sc[...], u], axis=1).reshape(sb * (SUBLANES + c), w)
    acc = cb_ref[...] + ext * cw_ref[3:4, :]
    for j in range(1, 4):
        acc = acc + pltpu.roll(ext, j, 0) * cw_ref[3 - j:4 - j, :]
    xc = acc.reshape(sb, SUBLANES + c, w)[:, SUBLANES:, :].reshape(sb * c, w)
    hist_sc[...] = u[:, c - SUBLANES:, :]

    xb = xc.astype(BF16)
    half = w // 2

    def gate(w_ref, b_ref):
        pre = jnp.concatenate([_dot(xb[:, :half], w_ref[0]), _dot(xb[:, half:], w_ref[1])], axis=1)
        return _sigmoid(pre + b_ref[...])

    r = gate(wr_ref, br_ref)
    i = gate(wi_ref, bi_ref)
    log_a = -LRU_C * r * _softplus(-lam_ref[...])
    a = jnp.exp(log_a)
    b = jnp.sqrt(1.0 - a * a) * (i * xc)

    t = _mod_pow2(lax.broadcasted_iota(jnp.int32, (sb * c, w), 0), c)
    s = 1
    while s < c:
        m = t >= s
        a_sh = pltpu.roll(a, s, 0)
        b_sh = pltpu.roll(b, s, 0)
        b = jnp.where(m, a * b_sh + b, b)
        a = jnp.where(m, a * a_sh, a)
        s *= 2

    h = a.reshape(sb, c, w) * h_sc[...] + b.reshape(sb, c, w)
    h_sc[...] = h[:, c - 1:c, :]
    ag = ag_ref[...].astype(F32)
    oa_ref[...] = (h * (ag * _sigmoid(ag))).astype(BF16)

    @pl.when(ci == pl.num_programs(1) - 1)
    def _():
        hlast_ref[...] = h_sc[...]


def _lru(ax, ag, cinit, h0, p, *, sb, c):
    n_seq, t, w = ax.shape
    kern = functools.partial(_lru_kernel, sb=sb, c=c)
    blk = lambda s: pl.BlockSpec(s, lambda i, j: (i, j, 0))
    return pl.pallas_call(
        kern,
        grid=(n_seq // sb, t // c),
        in_specs=[
            blk((sb, c, w)), blk((sb, c, w)),
            pl.BlockSpec((sb, SUBLANES, w), lambda i, j: (i, 0, 0)),
            pl.BlockSpec((sb, 1, w), lambda i, j: (i, 0, 0)),
            _const_spec((SUBLANES, w)), _const_spec((1, w)),
            _const_spec((2, w // 2, w // 2)), _const_spec((1, w)),
            _const_spec((2, w // 2, w // 2)), _const_spec((1, w)),
            _const_spec((1, w)),
        ],
        out_specs=[blk((sb, c, w)), pl.BlockSpec((sb, 1, w), lambda i, j: (i, 0, 0))],
        out_shape=[jax.ShapeDtypeStruct((n_seq, t, w), BF16), jax.ShapeDtypeStruct((n_seq, 1, w), F32)],
        scratch_shapes=[pltpu.VMEM((sb, SUBLANES, w), F32), pltpu.VMEM((sb, 1, w), F32)],
        compiler_params=_cparams(("arbitrary", "arbitrary")),
        name="rg_lru",
    )(ax, ag, cinit, h0, p["conv_w8"], p["conv_b"], p["lru_wr2"], p["lru_br"], p["lru_wi2"],
      p["lru_bi"], p["lru_lam"])


def _rwkv_kernel(bs_ref, bg_ref, pinit_ref, s0_ref, mu_ref, w0_ref, lora_ref, a0_ref, kkp_ref,
                 ka_ref, rk_ref, gng_ref, gnb_ref, bones_ref, ob_ref, sout_ref, hist_sc, s_sc,
                 *, sb, c):
    ci = pl.program_id(1)
    rows = sb * c
    hw = RW_WIDTH
    tb = RW_TB
    ntb = rows // tb
    spt = tb // c
    gt = RW_PAIRS * ntb
    gs = RW_PAIRS * sb

    @pl.when(ci == 0)
    def _():
        hist_sc[...] = pinit_ref[...]
        z = jnp.zeros((sb, RW_HD, RW_HD), F32)
        for p in range(RW_PAIRS):
            top = jnp.concatenate([s0_ref[:, 2 * p], z], axis=2)
            bot = jnp.concatenate([z, s0_ref[:, 2 * p + 1]], axis=2)
            s_sc[p] = jnp.concatenate([top, bot], axis=1)

    s3 = bs_ref[...]
    ext = jnp.concatenate([hist_sc[...], s3], axis=1).reshape(sb * (SUBLANES + c), SHIFT_COLS)
    sprev = pltpu.roll(ext, 1, 0).reshape(sb, SUBLANES + c, SHIFT_COLS)[:, SUBLANES:, :]
    sprev = sprev.reshape(rows, SHIFT_COLS)
    hist_sc[...] = s3[:, c - SUBLANES:, :]
    s = s3.reshape(rows, SHIFT_COLS)
    xs = s + (sprev - s) * mu_ref[...]
    r = xs[:, 0:hw]
    k = xs[:, hw:2 * hw]
    v = xs[:, 2 * hw:3 * hw]
    la = xs[:, 3 * hw:3 * hw + LANES]
    lane = lax.broadcasted_iota(jnp.int32, (rows, LANES), 1)
    lin = jnp.where(lane < RW_HD, jnp.tanh(la), la).astype(BF16)
    lo = _dot(lin, lora_ref[...])
    wlog = -_softplus(-(w0_ref[...] + lo[:, :hw])) - 0.5
    logw = -jnp.exp(wlog)
    a = _sigmoid(a0_ref[...] + lo[:, hw:])
    bones = bones_ref[...]
    kk = k * kkp_ref[...]
    kk = kk * lax.rsqrt(_dot_sel_r(kk * kk, bones) + 1e-12)
    kf = k * (1.0 + (a - 1.0) * ka_ref[...])
    b = kk * a

    ri = lax.broadcasted_iota(jnp.int32, (rows, rows), 0)
    cj = lax.broadcasted_iota(jnp.int32, (rows, rows), 1)
    same_seq = _div_pow2(ri, c) == _div_pow2(cj, c)
    ltri = jnp.where(same_seq & (cj <= ri), 1.0, 0.0).astype(BF16)
    lseq = jnp.where(same_seq, 1.0, 0.0).astype(BF16)
    log_g = _dot_sel_l3(ltri, logw)
    log_end = _dot_sel_l3(lseq, logw)
    g_inv = jnp.exp(-log_g)
    g_rem = jnp.exp(log_end - log_g)
    g_end = jnp.exp(log_end)

    low = lax.broadcasted_iota(jnp.int32, (1, 1, LANES), 2) < RW_HD

    def to_tb(x):
        return jnp.stack([x[tb * kb:tb * (kb + 1), LANES * p:LANES * (p + 1)]
                          for p in range(RW_PAIRS) for kb in range(ntb)], axis=0)

    def stack2(x3):
        return jnp.concatenate([jnp.where(low, x3, 0.0), jnp.where(low, 0.0, x3)], axis=1)

    def dup2(x3):
        return jnp.concatenate([x3, x3], axis=1)

    def to_ps(x3):
        if spt == 1:
            return x3
        return jnp.stack([jnp.concatenate([x3[g, c * q:c * (q + 1)], x3[g, tb + c * q:tb + c * (q + 1)]], axis=0)
                          for g in range(gt) for q in range(spt)], axis=0)

    def from_ps(y3):
        if spt == 1:
            return y3
        return jnp.stack([jnp.concatenate([y3[g * spt + q, :c] for q in range(spt)]
                                          + [y3[g * spt + q, c:] for q in range(spt)], axis=0)
                          for g in range(gt)], axis=0)

    xk = stack2(to_tb(kk * jnp.exp(log_g - logw)))
    xr = stack2(to_tb(r * jnp.exp(log_g)))
    vst = stack2(to_tb(v))
    kgs = stack2(to_tb(kf * g_rem))
    bgs = stack2(to_tb(b * g_rem))
    yb = dup2(to_tb(b * g_inv)).astype(BF16)
    yk = dup2(to_tb(kf * g_inv)).astype(BF16)
    xx = jnp.concatenate([xk, xr], axis=1).astype(BF16)
    ab = _bdot_nt(xx, yb)
    ak = _bdot_nt(xx, yk)

    n2 = 2 * tb
    i2 = lax.broadcasted_iota(jnp.int32, (1, n2, n2), 1)
    j2 = lax.broadcasted_iota(jnp.int32, (1, n2, n2), 2)
    same_blk = _div_pow2(i2, c) == _div_pow2(j2, c)
    m_strict = same_blk & (j2 < i2)
    m_incl = same_blk & (j2 <= i2)
    eye = jnp.where(i2 == j2, 1.0, 0.0)
    a_bk = jnp.where(m_strict, ab[:, :n2], 0.0)
    a_br = jnp.where(m_incl, ab[:, n2:], 0.0)
    a_kk = jnp.where(m_strict, ak[:, :n2], 0.0)
    a_kr = jnp.where(m_incl, ak[:, n2:], 0.0)

    pw = -a_bk
    tinv = eye + pw
    for _ in range(int(math.log2(c)) - 1):
        pw = _bdot16(pw, pw)
        tinv = tinv + _bdot16(tinv, pw)

    st = s_sc[...].reshape(gs, LANES, LANES)
    stb = st.astype(BF16)
    if spt == 1:
        res = _bdot_nt(xx, stb)
        xks, xrs = res[:, :n2], res[:, n2:]
    else:
        xks = from_ps(_bdot_nt(to_ps(xk).astype(BF16), stb))
        xrs = from_ps(_bdot_nt(to_ps(xr).astype(BF16), stb))
    vb = vst.astype(BF16)
    uu = _bdot16(tinv, xks + _bdot(a_kk.astype(BF16), vb))
    ost = xrs + _bdot(a_kr.astype(BF16), vb) - _bdot(a_br.astype(BF16), uu.astype(BF16))
    o3 = ost[:, :tb] + ost[:, tb:]

    lhs = jnp.concatenate([to_ps(vst), -to_ps(uu)], axis=1).astype(BF16)
    rhs = jnp.concatenate([to_ps(kgs), to_ps(bgs)], axis=1).astype(BF16)
    ds_ = _bdot_tn(lhs, rhs)
    gend = jnp.stack([g_end[c * q:c * q + 1, LANES * p:LANES * (p + 1)]
                      for p in range(RW_PAIRS) for q in range(sb)], axis=0)
    s_sc[...] = (st * gend + ds_).reshape(RW_PAIRS, sb, LANES, LANES)

    o = jnp.concatenate([jnp.concatenate([o3[p * ntb + kb] for kb in range(ntb)], axis=0)
                         for p in range(RW_PAIRS)], axis=1)
    inv_hd = 1.0 / RW_HD
    mean = _dot_sel_r(o, bones) * inv_hd
    d = o - mean
    var = _dot_sel_r(d * d, bones) * inv_hd
    on = d * lax.rsqrt(var + GN_EPS) * gng_ref[...] + gnb_ref[...]
    bonus = _dot_sel_r(r * kf * rk_ref[...], bones) * v
    g = bg_ref[...].astype(F32).reshape(rows, hw)
    ob_ref[...] = ((on + bonus) * (g * _sigmoid(g))).astype(BF16).reshape(sb, c, hw)

    @pl.when(ci == pl.num_programs(1) - 1)
    def _():
        for p in range(RW_PAIRS):
            sp = s_sc[p]
            sout_ref[:, 2 * p] = sp[:, :RW_HD, :RW_HD]
            sout_ref[:, 2 * p + 1] = sp[:, RW_HD:, RW_HD:]


def _rwkv(bs, bg, pinit, s0, p, *, sb, c):
    n_seq, t, _ = bs.shape
    hw = RW_WIDTH
    assert (sb * c) % RW_TB == 0 and RW_TB % c == 0
    kern = functools.partial(_rwkv_kernel, sb=sb, c=c)
    blk = lambda s: pl.BlockSpec(s, lambda i, j: (i, j, 0))
    st_spec = pl.BlockSpec((sb, 2 * RW_PAIRS, RW_HD, RW_HD), lambda i, j: (i, 0, 0, 0))
    return pl.pallas_call(
        kern,
        grid=(n_seq // sb, t // c),
        in_specs=[
            blk((sb, c, SHIFT_COLS)), blk((sb, c, hw)),
            pl.BlockSpec((sb, SUBLANES, SHIFT_COLS), lambda i, j: (i, 0, 0)),
            st_spec,
            _const_spec((1, SHIFT_COLS)), _const_spec((1, hw)), _const_spec((LANES, 2 * hw)),
            _const_spec((1, hw)), _const_spec((1, hw)), _const_spec((1, hw)), _const_spec((1, hw)),
            _const_spec((1, hw)), _const_spec((1, hw)), _const_spec((hw, hw)),
        ],
        out_specs=[blk((sb, c, hw)), st_spec],
        out_shape=[jax.ShapeDtypeStruct((n_seq, t, hw), BF16),
                   jax.ShapeDtypeStruct((n_seq, 2 * RW_PAIRS, RW_HD, RW_HD), F32)],
        scratch_shapes=[pltpu.VMEM((sb, SUBLANES, SHIFT_COLS), F32),
                        pltpu.VMEM((RW_PAIRS, sb, LANES, LANES), F32)],
        compiler_params=_cparams(("arbitrary", "arbitrary")),
        name="rwkv7",
    )(bs, bg, pinit, s0, p["rw_mu"], p["rw_w0"], p["rw_lora"], p["rw_a0"], p["rw_kk"], p["rw_ka"],
      p["rw_rk"], p["rw_gn_g"], p["rw_gn_b"], p["bones"])


def _rope128(x, cos, sin_a, sin_b):
    return x * cos + pltpu.roll(x, LANES - ROPE_DIM // 2, 1) * sin_a + pltpu.roll(x, ROPE_DIM // 2, 1) * sin_b


def _mla_prep_kernel(cq_ref, ckv_ref, pe_ref, cos_ref, sa_ref, sb_ref, gqa_ref, wq_ref, gq_ref,
                     gkva_ref, wk_ref, gk_ref, wvt_ref, ckvn_ref, q_ref, *kv_refs, sb, tt):
    rows = sb * tt
    cq = cq_ref[...].reshape(rows, Q_RANK)
    cqn = (cq * lax.rsqrt(jnp.mean(cq * cq, axis=-1, keepdims=True) + EPS) * gqa_ref[...]).astype(BF16)
    q = _dot(cqn, wq_ref[...])
    ckv = ckv_ref[...].reshape(rows, KV_RANK)
    ckvn = ckv * lax.rsqrt(jnp.mean(ckv * ckv, axis=-1, keepdims=True) + EPS) * gkva_ref[...]
    ckvn_ref[...] = ckvn.reshape(sb, tt, KV_RANK)

    def tile(ref):
        return jnp.broadcast_to(ref[...][None], (sb, tt, LANES)).reshape(rows, LANES)

    cos, sa, sb_ = tile(cos_ref), tile(sa_ref), tile(sb_ref)
    inv_d = 1.0 / QK_DIM
    for h in range(MLA_HEADS):
        qh = q[:, h * LANES:(h + 1) * LANES]
        qn = qh * lax.rsqrt(jnp.sum(qh * qh, axis=-1, keepdims=True) * inv_d + EPS) * gq_ref[...]
        q_ref[:, h] = (_rope128(qn, cos, sa, sb_) * SCALE).astype(BF16).reshape(sb, tt, LANES)
    if not kv_refs:
        return
    k_ref, vt_ref = kv_refs
    cb = ckvn.astype(BF16)
    kvk = _dot(cb, wk_ref[...])
    vt = _dot_nt(wvt_ref[...], cb)
    pe = pe_ref[...].reshape(rows, LANES)
    extra = lax.broadcasted_iota(jnp.int32, (VT_ROWS - V_DIM, rows), 0)
    ones_rows = jnp.where(extra == 0, 1.0, 0.0)
    for h in range(MLA_HEADS):
        kh = kvk[:, h * LANES:(h + 1) * LANES] + pe
        kn = kh * lax.rsqrt(jnp.sum(kh * kh, axis=-1, keepdims=True) * inv_d + EPS) * gk_ref[...]
        k_ref[:, h] = _rope128(kn, cos, sa, sb_).astype(BF16).reshape(sb, tt, LANES)
        vt_ref[0, h] = jnp.concatenate([vt[h * V_DIM:(h + 1) * V_DIM], ones_rows], axis=0).astype(BF16)


def _mla_prep(cq, ckv, pe_a, tabs, p, *, sb, tt, with_kv):
    n_seq, t, _ = cq.shape
    assert sb == 1 or not with_kv
    kern = functools.partial(_mla_prep_kernel, sb=sb, tt=tt)
    blk = lambda w: pl.BlockSpec((sb, tt, w), lambda i, j: (i, j, 0))
    tab = pl.BlockSpec((tt, LANES), lambda i, j: (j, 0))
    hd = pl.BlockSpec((sb, MLA_HEADS, tt, LANES), lambda i, j: (i, 0, j, 0))
    hshape = jax.ShapeDtypeStruct((n_seq, MLA_HEADS, t, LANES), BF16)
    out_specs = [blk(KV_RANK), hd]
    out_shape = [jax.ShapeDtypeStruct((n_seq, t, KV_RANK), F32), hshape]
    if with_kv:
        out_specs += [hd, pl.BlockSpec((sb, MLA_HEADS, VT_ROWS, tt), lambda i, j: (i, 0, 0, j))]
        out_shape += [hshape, jax.ShapeDtypeStruct((n_seq, MLA_HEADS, VT_ROWS, t), BF16)]
    return pl.pallas_call(
        kern,
        grid=(n_seq // sb, t // tt),
        in_specs=[
            blk(Q_RANK), blk(KV_RANK), blk(LANES), tab, tab, tab,
            _const_spec((1, Q_RANK)), _const_spec((Q_RANK, MLA_HEADS * LANES)), _const_spec((1, LANES)),
            _const_spec((1, KV_RANK)), _const_spec((KV_RANK, MLA_HEADS * LANES)), _const_spec((1, LANES)),
            _const_spec((MLA_WIDTH, KV_RANK)),
        ],
        out_specs=out_specs,
        out_shape=out_shape,
        compiler_params=_cparams(("arbitrary", "arbitrary")),
        name="mla_prep",
    )(cq, ckv, pe_a, tabs[0], tabs[1], tabs[2], p["g_qa"], p["wq"], p["gq128"], p["g_kva"],
      p["wk"], p["gk128"], p["wvt"])


def _flash_kernel(q_ref, k_ref, vt_ref, o_ref, *, tq, tk):
    qi = pl.program_id(2)
    nsub = tq // tk
    qs = [q_ref[0, hh] for hh in range(2)]

    def tile(hh, j, carry, q0, mask):
        m, acc = carry
        kj = k_ref[0, hh, pl.ds(j * tk, tk), :]
        vj = vt_ref[0, hh, :, pl.ds(j * tk, tk)].astype(F32)
        s = _dot_nt(kj, qs[hh][q0:])
        if mask is not None:
            s = jnp.where(mask, s, NEG)
        m_old = m[:, q0:]
        m_new = jnp.maximum(m_old, jnp.max(s, axis=0, keepdims=True))
        pr = jnp.exp(s - m_new)
        acc_new = acc[:, q0:] * jnp.exp(m_old - m_new) + _dot(vj, pr)
        if q0:
            m_new = jnp.concatenate([m[:, :q0], m_new], axis=1)
            acc_new = jnp.concatenate([acc[:, :q0], acc_new], axis=1)
        return m_new, acc_new

    def step(j, carry):
        return tuple(tile(hh, j, carry[hh], 0, None) for hh in range(2))

    init1 = (jnp.full((1, tq), NEG, F32), jnp.zeros((VT_ROWS, tq), F32))
    carry = lax.fori_loop(0, qi * nsub, step, (init1, init1))
    for d in range(nsub):
        q0 = d * tk
        krow = lax.broadcasted_iota(jnp.int32, (tk, tq - q0), 0)
        qcol = lax.broadcasted_iota(jnp.int32, (tk, tq - q0), 1)
        carry = tuple(tile(hh, qi * nsub + d, carry[hh], q0, krow <= qcol) for hh in range(2))
    ot = jnp.concatenate([carry[hh][1][:V_DIM] / carry[hh][1][V_DIM:V_DIM + 1] for hh in range(2)],
                         axis=0)
    o_ref[0] = ot.T.astype(BF16)


def _flash(q, k, vt, *, tq, tk):
    b, h, t, _ = q.shape
    kern = functools.partial(_flash_kernel, tq=tq, tk=tk)
    return pl.pallas_call(
        kern,
        grid=(b, h // 2, t // tq),
        in_specs=[
            pl.BlockSpec((1, 2, tq, LANES), lambda bi, pi, qi: (bi, pi, qi, 0)),
            pl.BlockSpec((1, 2, t, LANES), lambda bi, pi, qi: (bi, pi, 0, 0)),
            pl.BlockSpec((1, 2, VT_ROWS, t), lambda bi, pi, qi: (bi, pi, 0, 0)),
        ],
        out_specs=pl.BlockSpec((1, tq, LANES), lambda bi, pi, qi: (bi, qi, pi)),
        out_shape=jax.ShapeDtypeStruct((b, t, (h // 2) * LANES), BF16),
        compiler_params=_cparams(("arbitrary", "arbitrary", "arbitrary")),
        name="mla_flash",
    )(q, k, vt)


def _decode_kernel(pt_ref, q_ref, cnew_ref, penew_ref, ckv_hbm, kpe_hbm, wukt_ref, gkn_ref, sel_ref,
                   wukp_ref, e16_ref, gkr_ref, gkrt_ref, ct_ref, st_ref, tn_ref, wuv_ref, o_ref,
                   cbuf, pbuf, s_sc, cb_sc, sem, *, layer, pc, n_chunks, ts):
    b = pl.program_id(0)
    nb = pl.num_programs(0)
    page = ckv_hbm.shape[2]
    hq = MLA_HEADS * ts
    half = ROPE_DIM // 2

    def copies(bb, ch, slot):
        out = []
        for i in range(pc):
            phys = pt_ref[bb, ch * pc + i]
            out.append(pltpu.make_async_copy(
                ckv_hbm.at[layer, phys], cbuf.at[slot, pl.ds(i * page, page), :], sem.at[0, slot]))
            out.append(pltpu.make_async_copy(
                kpe_hbm.at[layer, phys], pbuf.at[slot, :, pl.ds(i * page, page)], sem.at[1, slot]))
        return out

    @pl.when(b == 0)
    def _():
        for cp in copies(0, 0, 0):
            cp.start()

    qf = q_ref[0].astype(F32).reshape(hq, LANES)
    qg = qf * gkn_ref[...]
    qlat = jnp.concatenate(
        [_dot(qg[h * ts:(h + 1) * ts].astype(BF16), wukt_ref[h]) for h in range(MLA_HEADS)],
        axis=0).astype(BF16)
    qrope = _dot(qf.astype(BF16), sel_ref[...])
    qrope32 = qrope[:, :ROPE_DIM].astype(BF16)
    qrope128 = qrope.astype(BF16)
    e16 = e16_ref[...]
    inv_d = 1.0 / QK_DIM
    n_col = MLA_HEADS * NOPE_DIM // LANES

    def nope_ssq(cb):
        kn = _dot(cb, wukp_ref[...])
        part = kn[:, 0:LANES] * kn[:, 0:LANES]
        for cc in range(1, n_col):
            kc = kn[:, cc * LANES:(cc + 1) * LANES]
            part = part + kc * kc
        return _dot_nt(e16, part.astype(BF16))

    def update(carry, s, cb):
        m, l, acc = carry
        m_new = jnp.maximum(m, jnp.max(s, axis=-1, keepdims=True))
        pr = jnp.exp(s - m_new)
        corr = jnp.exp(m - m_new)
        l = l * corr + jnp.sum(pr, axis=-1, keepdims=True)
        acc = acc * corr + _dot(pr.astype(BF16), cb)
        return m_new, l, acc

    def dma_step(ch):
        slot = (b * n_chunks + ch) % 2
        for cp in copies(b, ch, slot):
            cp.wait()

        @pl.when(ch + 1 < n_chunks)
        def _():
            for cp in copies(b, ch + 1, 1 - slot):
                cp.start()

        @pl.when((ch + 1 == n_chunks) & (b + 1 < nb))
        def _():
            for cp in copies(b + 1, 0, 1 - slot):
                cp.start()
        return slot

    def stage_a(ch, slot, par):
        krs, pss = [], []
        for i in range(pc):
            pg = ch * pc + i
            x = pbuf[slot, :, i * page:(i + 1) * page]
            xg = x * gkrt_ref[...]
            sw = jnp.concatenate([xg[half:], xg[:half]], axis=0)
            krs.append((xg * ct_ref[pg] + sw * st_ref[pg]).astype(BF16))
            pss.append(jnp.sum(x * x, axis=0, keepdims=True))
        cb = cbuf[slot].astype(BF16)
        kr = jnp.concatenate(krs, axis=1)
        pesq = jnp.concatenate(pss, axis=1)
        ssq = nope_ssq(cb) + pesq
        s = _dot_nt(qlat, cb) + _dot(qrope32, kr)
        s_sc[par] = s * lax.rsqrt(ssq * inv_d + EPS)
        cb_sc[par] = cb

    def step(ch, par, carry):
        slot = dma_step(ch)
        stage_a(ch, slot, par)
        return update(carry, s_sc[1 - par], cb_sc[1 - par])

    init = (jnp.full((hq, 1), NEG, F32), jnp.zeros((hq, 1), F32), jnp.zeros((hq, LANES), F32))
    stage_a(0, dma_step(0), 0)
    n_steps = n_chunks - 1

    def pair(j, carry):
        carry = step(2 * j + 1, 1, carry)
        return step(2 * j + 2, 0, carry)

    carry = lax.fori_loop(0, n_steps // 2, pair, init)
    if n_steps % 2:
        carry = step(n_chunks - 1, 1, carry)
    last = (n_chunks - 1) % 2
    carry = update(carry, s_sc[last], cb_sc[last])

    pad = jnp.zeros((NEW_ROWS - ts, LANES), F32)
    cn = jnp.concatenate([cnew_ref[0], pad], axis=0)
    pn = jnp.concatenate([penew_ref[0], pad], axis=0)
    krn = _rope128(pn * gkr_ref[...], tn_ref[0], tn_ref[1], tn_ref[2]).astype(BF16)
    cnb = cn.astype(BF16)
    ones_r = jnp.ones((hq, LANES), BF16)
    ssq = nope_ssq(cnb) + _dot_nt(ones_r, (pn * pn).astype(BF16))
    s = (_dot_nt(qlat, cnb) + _dot_nt(qrope128, krn)) * lax.rsqrt(ssq * inv_d + EPS)
    qidx = _mod_pow2(lax.broadcasted_iota(jnp.int32, (hq, NEW_ROWS), 0), ts)
    tidx = lax.broadcasted_iota(jnp.int32, (hq, NEW_ROWS), 1)
    s = jnp.where(tidx <= qidx, s, NEG)
    _, l, acc = update(carry, s, cnb)
    al = acc / l
    out = _dot(al[0:ts].astype(BF16), wuv_ref[0])
    for h in range(1, MLA_HEADS):
        out = out + _dot(al[h * ts:(h + 1) * ts].astype(BF16), wuv_ref[h])
    o_ref[0] = out


def _decode(page_table, q, cnew, penew, cache_ckv, cache_kpe_t, tabs_c, tabs_s, tabs_n, p, *, layer, pc):
    db, h, ts, _ = q.shape
    n_pages = page_table.shape[1]
    page = cache_ckv.shape[2]
    assert n_pages % pc == 0
    n_chunks = n_pages // pc
    tok = pc * page
    kern = functools.partial(_decode_kernel, layer=layer, pc=pc, n_chunks=n_chunks, ts=ts)
    cs = lambda shape: pl.BlockSpec(shape, lambda b, pt: (0,) * len(shape))
    cs1 = lambda shape: pl.BlockSpec(shape, lambda b, pt: (0,) * len(shape), pipeline_mode=pl.Buffered(1))
    grid_spec = pltpu.PrefetchScalarGridSpec(
        num_scalar_prefetch=1,
        grid=(db,),
        in_specs=[
            pl.BlockSpec((1, h, ts, LANES), lambda b, pt: (b, 0, 0, 0)),
            pl.BlockSpec((1, ts, LANES), lambda b, pt: (b, 0, 0)),
            pl.BlockSpec((1, ts, LANES), lambda b, pt: (b, 0, 0)),
            pl.BlockSpec(memory_space=pl.ANY),
            pl.BlockSpec(memory_space=pl.ANY),
            cs((h, LANES, LANES)), cs((1, LANES)), cs((LANES, LANES)),
            cs((KV_RANK, h * NOPE_DIM)), cs((h * ts, LANES)), cs((1, LANES)), cs((ROPE_DIM, LANES)),
            cs1((n_pages, ROPE_DIM, page)), cs1((n_pages, ROPE_DIM, page)), cs((3, NEW_ROWS, LANES)),
            cs((h, KV_RANK, MLA_WIDTH)),
        ],
        out_specs=pl.BlockSpec((1, ts, MLA_WIDTH), lambda b, pt: (b, 0, 0)),
        scratch_shapes=[
            pltpu.VMEM((2, tok, KV_RANK), F32),
            pltpu.VMEM((2, ROPE_DIM, tok), F32),
            pltpu.VMEM((2, h * ts, tok), F32),
            pltpu.VMEM((2, tok, KV_RANK), BF16),
            pltpu.SemaphoreType.DMA((2, 2)),
        ],
    )
    return pl.pallas_call(
        kern,
        grid_spec=grid_spec,
        out_shape=jax.ShapeDtypeStruct((db, ts, MLA_WIDTH), F32),
        compiler_params=_cparams(("arbitrary",)),
        name="mla_decode",
    )(page_table, q, cnew, penew, cache_ckv, cache_kpe_t, p["wukt"], p["gkn128"], p["sel_rope"],
      p["wukp"], p["e16"], p["gkr128"], p["gkrt"], tabs_c, tabs_s, tabs_n, p["wuv"])


def _merge_kernel(x_ref, gate_ref, oa_ref, ob_ref, oc_ref, cg_ref, woa_ref, wob_ref, woc_ref,
                  wout_ref, y_ref):
    d = x_ref.shape[1]
    cg = cg_ref[...].astype(F32)
    oc = (oc_ref[...].astype(F32) * (cg * _sigmoid(cg))).astype(BF16)
    g = _sigmoid(gate_ref[...].astype(F32))
    merged = (g[:, 0:d] * _dot(oa_ref[...], woa_ref[...])
              + g[:, d:2 * d] * _dot(ob_ref[...], wob_ref[...])
              + g[:, 2 * d:3 * d] * _dot(oc, woc_ref[...]))
    y_ref[...] = x_ref[...] + _dot(merged.astype(BF16), wout_ref[...])


def _merge(x2d, gate, oa, ob, oc, cg, p):
    n, d = x2d.shape
    tm = min(512, n)
    row = lambda w: pl.BlockSpec((tm, w), lambda i: (i, 0))
    return pl.pallas_call(
        _merge_kernel,
        grid=(n // tm,),
        in_specs=[row(d), row(3 * d), row(LRU_WIDTH), row(RW_WIDTH), row(MLA_WIDTH), row(MLA_WIDTH),
                  _const_spec((LRU_WIDTH, d)), _const_spec((RW_WIDTH, d)), _const_spec((MLA_WIDTH, d)),
                  _const_spec((d, d))],
        out_specs=row(d),
        out_shape=jax.ShapeDtypeStruct((n, d), F32),
        compiler_params=_cparams(("arbitrary",)),
        name="merge",
    )(x2d, gate, oa, ob, oc, cg, p["w_oa"], p["w_ob"], p["w_oc"], p["w_out"])


def _rope_angles(pos):
    inv = 1.0 / (ROPE_THETA ** (jnp.arange(0, ROPE_DIM, 2, dtype=F32) / ROPE_DIM))
    ang = pos.astype(F32)[:, None] * inv[None, :]
    return jnp.cos(ang), jnp.sin(ang)


def _rope_pattern(pos, lane0, rows=None):
    half = ROPE_DIM // 2
    cos, sin = _rope_angles(pos)
    n = pos.shape[0]
    ones_before = jnp.ones((n, lane0), F32) if lane0 else jnp.zeros((n, 0), F32)
    z = lambda w: jnp.zeros((n, w), F32)
    rest = LANES - lane0 - ROPE_DIM
    cos_t = jnp.concatenate([ones_before, cos, cos, z(rest)], axis=1)
    sin_a = jnp.concatenate([z(lane0), -sin, z(half), z(rest)], axis=1)
    sin_b = jnp.concatenate([z(lane0), z(half), sin, z(rest)], axis=1)
    tabs = jnp.stack([cos_t, sin_a, sin_b], axis=0)
    if rows is not None and rows > n:
        tabs = jnp.pad(tabs, ((0, 0), (0, rows - n), (0, 0)))
    return tabs


def _rope_pages(n_pages, page):
    cos, sin = _rope_angles(jnp.arange(n_pages * page))
    ct = jnp.concatenate([cos, cos], axis=1)
    st = jnp.concatenate([-sin, sin], axis=1)
    lay = lambda x: jnp.transpose(x.reshape(n_pages, page, ROPE_DIM), (0, 2, 1))
    return lay(ct), lay(st)


def _block_diag2(w):
    nb, bs, _ = w.shape
    per = nb // 2
    out = jnp.zeros((2, per * bs, per * bs), w.dtype)
    for i in range(nb):
        hf, j = divmod(i, per)
        out = out.at[hf, j * bs:(j + 1) * bs, j * bs:(j + 1) * bs].set(w[i])
    return out


def _layer_params(l, ts, a):
    d = a["w_in"].shape[1]
    w = a["w_in"][l]
    src_pe = SEG_CKV[1]
    z = lambda n: jnp.zeros((d, n), w.dtype)
    w_pad = jnp.concatenate([
        w[:, :src_pe],
        z(64), w[:, src_pe:src_pe + ROPE_DIM], z(32),
        w[:, src_pe:src_pe + ROPE_DIM], z(96),
        w[:, src_pe + ROPE_DIM:],
    ], axis=1).astype(BF16)
    assert w_pad.shape[1] == IN_COLS_PAD
    row = lambda v: v.reshape(1, -1)
    hd = RW_HD
    bones = jnp.kron(jnp.eye(RW_WIDTH // hd, dtype=F32), jnp.ones((hd, hd), F32)).astype(BF16)
    zl = jnp.zeros((LORA, RW_WIDTH), F32)
    lora = jnp.concatenate([jnp.concatenate([a["rw_w2"][l], zl], axis=1),
                            jnp.concatenate([zl, a["rw_a2"][l]], axis=1)], axis=0).astype(BF16)
    w_uq = a["mla_w_uq"][l]
    w_ukv = a["mla_w_ukv"][l]
    w_uk, w_uv = w_ukv[..., :NOPE_DIM], w_ukv[..., NOPE_DIM:]
    g_kn = a["mla_g_kn"][l]
    pad_h = lambda x: jnp.pad(x, ((0, 0), (0, 0), (0, LANES - x.shape[-1]))).reshape(x.shape[0], -1)
    wukt = jnp.pad(jnp.transpose(w_uk, (1, 2, 0)), ((0, 0), (0, LANES - NOPE_DIM), (0, 0)))
    wukp = jnp.transpose(w_uk.reshape(KV_RANK, MLA_HEADS, NOPE_DIM // 16, 16), (0, 2, 1, 3))
    wukp = wukp.reshape(KV_RANK, MLA_HEADS * NOPE_DIM)
    e16 = jnp.repeat(jnp.kron(jnp.eye(MLA_HEADS, dtype=F32), jnp.ones((1, 16), F32)), ts, axis=0)
    wuv = jnp.zeros((MLA_HEADS, KV_RANK, MLA_WIDTH), F32)
    for h in range(MLA_HEADS):
        wuv = wuv.at[h, :, h * V_DIM:(h + 1) * V_DIM].set(w_uv[:, h, :])
    sel = jnp.zeros((LANES, LANES), F32).at[
        jnp.arange(NOPE_DIM, QK_DIM), jnp.arange(ROPE_DIM)].set(1.0)
    return {
        "norm_g": row(a["norm_g"][l]), "w_in": w_pad,
        "conv_w8": jnp.pad(a["conv_w"][l], ((0, SUBLANES - a["conv_w"].shape[1]), (0, 0))),
        "conv_b": row(a["conv_b"][l]),
        "lru_wr2": _block_diag2(a["lru_wr"][l]).astype(BF16), "lru_br": row(a["lru_br"][l]),
        "lru_wi2": _block_diag2(a["lru_wi"][l]).astype(BF16), "lru_bi": row(a["lru_bi"][l]),
        "lru_lam": row(a["lru_lam"][l]),
        "rw_mu": row(a["rw_mu"][l]), "rw_w0": row(a["rw_w0"][l]), "rw_lora": lora,
        "rw_a0": row(a["rw_a0"][l]), "rw_kk": row(a["rw_kk"][l]), "rw_ka": row(a["rw_ka"][l]),
        "rw_rk": row(a["rw_rk"][l]), "rw_gn_g": row(a["rw_gn_g"][l]), "rw_gn_b": row(a["rw_gn_b"][l]),
        "bones": bones,
        "g_qa": row(a["mla_g_qa"][l]), "wq": pad_h(w_uq).astype(BF16),
        "gq128": row(jnp.pad(a["mla_g_qn"][l], (0, LANES - QK_DIM))),
        "g_kva": row(a["mla_g_kva"][l]), "wk": pad_h(w_uk).astype(BF16),
        "gk128": row(jnp.pad(g_kn, (0, LANES - QK_DIM))),
        "wvt": jnp.transpose(w_uv.reshape(KV_RANK, MLA_WIDTH)).astype(BF16),
        "wukt": wukt.astype(BF16),
        "gkn128": row(jnp.pad(g_kn[:NOPE_DIM], (0, LANES - NOPE_DIM))),
        "sel_rope": sel.astype(BF16), "wukp": wukp.astype(BF16), "e16": e16.astype(BF16),
        "gkr128": row(jnp.pad(g_kn[NOPE_DIM:], (0, LANES - ROPE_DIM))),
        "gkrt": jnp.broadcast_to(g_kn[NOPE_DIM:, None], (ROPE_DIM, LANES)),
        "wuv": wuv.astype(BF16),
        "w_oa": a["w_oa"][l].astype(BF16), "w_ob": a["w_ob"][l].astype(BF16),
        "w_oc": a["w_oc"][l].astype(BF16), "w_out": a["w_out"][l].astype(BF16),
    }


def _hist_rows(x):
    return jnp.pad(x, ((0, 0), (SUBLANES - x.shape[1], 0), (0, 0)))


def _mixer_layer(x, p, lru_h0, conv_buf, rw_s0, rw_prev, tabs, attend, *, lru_blk, rw_blk, prep_blk,
                 with_kv):
    n_seq, t, d = x.shape
    n = n_seq * t
    segs = _inproj(x.reshape(n, d), p["norm_g"], p["w_in"])
    ax, ag, bs, bg, cq, ckv_raw, pe_a, pe_b, cg, gate = segs
    r3 = lambda v: v.reshape(n_seq, t, v.shape[-1])
    oa, lru_h = _lru(r3(ax), r3(ag), _hist_rows(conv_buf), lru_h0[:, None, :], p, sb=lru_blk[0],
                     c=lru_blk[1])
    lru_h = lru_h[:, 0, :]
    ob, rw_s = _rwkv(r3(bs), r3(bg), _hist_rows(rw_prev[:, None, :]), rw_s0, p, sb=rw_blk[0], c=rw_blk[1])
    prep = _mla_prep(r3(cq), r3(ckv_raw), r3(pe_a), tabs, p, sb=prep_blk[0], tt=prep_blk[1],
                     with_kv=with_kv)
    ckv = prep[0]
    oc = attend(prep, r3(pe_b))
    y = _merge(x.reshape(n, d), gate, oa.reshape(n, -1), ob.reshape(n, -1), oc.reshape(n, -1), cg, p)
    conv_new = r3(ax)[:, t - conv_buf.shape[1]:, :]
    shift_new = r3(bs)[:, t - 1, :]
    kpe = r3(pe_b)[:, :, :ROPE_DIM]
    return y.reshape(n_seq, t, d), (lru_h, conv_new, rw_s, shift_new, ckv, kpe)


def kernel(x_prompt, x_sample, state_lru_h, state_lru_conv, state_rwkv_S, state_rwkv_shift, cache_ckv, cache_kpe, page_table, norm_g, w_in, conv_w, conv_b, lru_wr, lru_br, lru_wi, lru_bi, lru_lam, rw_mu, rw_w0, rw_w2, rw_a0, rw_a2, rw_kk, rw_ka, rw_rk, rw_gn_g, rw_gn_b, mla_g_qa, mla_g_kva, mla_w_uq, mla_w_ukv, mla_g_qn, mla_g_kn, w_oa, w_ob, w_oc, w_out):
    a = dict(norm_g=norm_g, w_in=w_in, conv_w=conv_w, conv_b=conv_b, lru_wr=lru_wr, lru_br=lru_br,
             lru_wi=lru_wi, lru_bi=lru_bi, lru_lam=lru_lam, rw_mu=rw_mu, rw_w0=rw_w0, rw_w2=rw_w2,
             rw_a0=rw_a0, rw_a2=rw_a2, rw_kk=rw_kk, rw_ka=rw_ka, rw_rk=rw_rk, rw_gn_g=rw_gn_g,
             rw_gn_b=rw_gn_b, mla_g_qa=mla_g_qa, mla_g_kva=mla_g_kva, mla_w_uq=mla_w_uq,
             mla_w_ukv=mla_w_ukv, mla_g_qn=mla_g_qn, mla_g_kn=mla_g_kn, w_oa=w_oa, w_ob=w_ob,
             w_oc=w_oc, w_out=w_out)
    bsz, t, _ = x_prompt.shape
    db, ts, _ = x_sample.shape
    depth = w_in.shape[0]
    n_pages = page_table.shape[1]
    page = cache_ckv.shape[2]
    past = n_pages * page
    dt = x_prompt.dtype
    assert ts == SUBLANES and t % RW_TB == 0 and db % SUBLANES == 0

    tabs_p = _rope_pattern(jnp.arange(t), NOPE_DIM)
    tabs_s = _rope_pattern(past + jnp.arange(ts), NOPE_DIM)
    dec_c, dec_s = _rope_pages(n_pages, page)
    dec_n = _rope_pattern(past + jnp.arange(ts), 0, rows=NEW_ROWS)
    cache_kpe_t = jnp.transpose(cache_kpe, (0, 1, 3, 2))

    tq = min(FLASH_TQ, t)
    tk = min(FLASH_TK, t)
    pc = min(DECODE_PAGES, n_pages)
    lru_c = min(256, t)
    prep_tt = min(512, t)
    sb_s = min(32, db)

    hp, hs = x_prompt, x_sample
    st_p, st_s = [], []
    for l in range(depth):
        p = _layer_params(l, ts, a)

        def attend_prompt(prep, pe_b):
            _, q, k, vt = prep
            return _flash(q, k, vt, tq=tq, tk=tk)

        def attend_sample(prep, pe_b, l=l, p=p):
            ckv, q = prep
            return _decode(page_table, q, ckv, pe_b, cache_ckv, cache_kpe_t, dec_c, dec_s, dec_n, p,
                           layer=l, pc=pc)

        hp, sp = _mixer_layer(
            hp, p, jnp.zeros((bsz, LRU_WIDTH), dt), jnp.zeros((bsz, 3, LRU_WIDTH), dt),
            jnp.zeros((bsz, 2 * RW_PAIRS, RW_HD, RW_HD), dt),
            jnp.zeros((bsz, SHIFT_COLS), dt), tabs_p, attend_prompt, lru_blk=(bsz, lru_c),
            rw_blk=(bsz, RW_TB), prep_blk=(1, prep_tt), with_kv=True)
        hs, ss = _mixer_layer(
            hs, p, state_lru_h[l], state_lru_conv[l], state_rwkv_S[l], state_rwkv_shift[l],
            tabs_s, attend_sample, lru_blk=(sb_s, ts), rw_blk=(RW_TB // ts, ts),
            prep_blk=(sb_s, ts), with_kv=False)
        st_p.append(sp)
        st_s.append(ss)

    stack = lambda states, i: jnp.stack([s[i] for s in states], axis=0)
    return (hp, hs) + tuple(stack(st_p, i) for i in range(6)) + tuple(stack(st_s, i) for i in range(6))
```

```python
import functools
import math

import jax
import jax.numpy as jnp
from jax import lax
from jax.experimental import pallas as pl
from jax.experimental.pallas import tpu as pltpu

F32 = jnp.float32
BF16 = jnp.bfloat16

EPS = 1e-6
LRU_C = 8.0
GN_EPS = 64e-5
ROPE_THETA = 10000.0
NEG = -1e30

LANES = 128
SUBLANES = 8
VMEM_LIMIT_BYTES = 56 * 1024 * 1024

LRU_WIDTH = 512
RW_WIDTH = 512
RW_HD = 64
RW_PAIRS = RW_WIDTH // LANES
LORA = 64
SHIFT_COLS = 3 * RW_WIDTH + 2 * LORA
Q_RANK = 256
KV_RANK = 128
MLA_HEADS = 8
NOPE_DIM = 64
ROPE_DIM = 32
V_DIM = 64
QK_DIM = NOPE_DIM + ROPE_DIM
MLA_WIDTH = MLA_HEADS * V_DIM
SCALE = QK_DIM ** -0.5

RW_TB = 64
FLASH_TQ = 2048
FLASH_TK = 512
DECODE_PAGES = 32
NEW_ROWS = 16
VT_ROWS = 80

SEG_AX = (0, 512)
SEG_AG = (512, 1024)
SEG_BS = (1024, 2688)
SEG_BG = (2688, 3200)
SEG_CQ = (3200, 3456)
SEG_CKV = (3456, 3584)
SEG_PEA = (3584, 3712)
SEG_PEB = (3712, 3840)
SEG_CG = (3840, 4352)
SEG_GATE = (4352, 7424)
IN_SEGS = (SEG_AX, SEG_AG, SEG_BS, SEG_BG, SEG_CQ, SEG_CKV, SEG_PEA, SEG_PEB, SEG_CG, SEG_GATE)
IN_DTYPES = (F32, BF16, F32, BF16, F32, F32, F32, F32, BF16, BF16)
IN_COLS_PAD = SEG_GATE[1]


def _cparams(sem):
    return pltpu.CompilerParams(dimension_semantics=sem, vmem_limit_bytes=VMEM_LIMIT_BYTES)


def _const_spec(shape):
    nd = len(shape)
    return pl.BlockSpec(shape, lambda *_: (0,) * nd)


def _sigmoid(x):
    return jax.nn.sigmoid(x)


def _softplus(x):
    return jnp.maximum(x, 0.0) + jnp.log(1.0 + jnp.exp(-jnp.abs(x)))


def _mod_pow2(x, n):
    return jnp.bitwise_and(x, n - 1)


def _div_pow2(x, n):
    return lax.shift_right_logical(x, int(math.log2(n)))


def _dot(a, b):
    return jnp.dot(a, b, preferred_element_type=F32)


def _dot_nt(a, b):
    return lax.dot_general(a, b, (((1,), (1,)), ((), ())), preferred_element_type=F32)


def _bdot(a, b):
    return jnp.einsum('gij,gjk->gik', a, b, preferred_element_type=F32)


def _bdot_nt(a, b):
    return jnp.einsum('gik,gjk->gij', a, b, preferred_element_type=F32)


def _bdot_tn(a, b):
    return jnp.einsum('gti,gtj->gij', a, b, preferred_element_type=F32)


def _split2(x):
    hi = x.astype(BF16)
    lo = (x - hi.astype(F32)).astype(BF16)
    return hi, lo


def _split3(x):
    hi = x.astype(BF16)
    r1 = x - hi.astype(F32)
    mid = r1.astype(BF16)
    lo = (r1 - mid.astype(F32)).astype(BF16)
    return hi, mid, lo


def _dot_sel_r(x, m01):
    hi, lo = _split2(x)
    return _dot(hi, m01) + _dot(lo, m01)


def _dot_sel_l3(m01, x):
    hi, mid, lo = _split3(x)
    return _dot(m01, hi) + _dot(m01, mid) + _dot(m01, lo)


def _bdot16(a, b):
    return _bdot(a.astype(BF16), b.astype(BF16))


def _inproj_kernel(x_ref, g_ref, w_ref, *out_refs):
    x = x_ref[...]
    ms = jnp.mean(x * x, axis=-1, keepdims=True)
    hn = (x * lax.rsqrt(ms + EPS) * g_ref[...]).astype(BF16)
    for o_ref, (lo, hi) in zip(out_refs, IN_SEGS):
        for c0 in range(lo, hi, 512):
            c1 = min(c0 + 512, hi)
            o_ref[:, c0 - lo:c1 - lo] = _dot(hn, w_ref[:, c0:c1]).astype(o_ref.dtype)


def _inproj(x2d, g, w_pad):
    n, d = x2d.shape
    tm = min(256, n)
    out_shape = [jax.ShapeDtypeStruct((n, hi - lo), dt) for (lo, hi), dt in zip(IN_SEGS, IN_DTYPES)]
    out_specs = [pl.BlockSpec((tm, hi - lo), lambda i: (i, 0)) for lo, hi in IN_SEGS]
    return pl.pallas_call(
        _inproj_kernel,
        grid=(n // tm,),
        in_specs=[
            pl.BlockSpec((tm, d), lambda i: (i, 0)),
            _const_spec((1, d)),
            pl.BlockSpec((d, IN_COLS_PAD), lambda i: (0, 0), pipeline_mode=pl.Buffered(1)),
        ],
        out_specs=out_specs,
        out_shape=out_shape,
        compiler_params=_cparams(("arbitrary",)),
        name="inproj",
    )(x2d, g, w_pad)


def _lru_kernel(ax_ref, ag_ref, cinit_ref, h0_ref, cw_ref, cb_ref, wr_ref, br_ref, wi_ref, bi_ref,
                lam_ref, oa_ref, hlast_ref, hist_sc, h_sc, *, sb, c):
    ci = pl.program_id(1)

    @pl.when(ci == 0)
    def _():
        hist_sc[...] = cinit_ref[...]
        h_sc[...] = h0_ref[...]

    w = LRU_WIDTH
    u = ax_ref[...]
    ext = jnp.concatenate([hist_sc[...], u], axis=1).reshape(sb * (SUBLANES + c), w)
    acc = cb_ref[...] + ext * cw_ref[3:4, :]
    for j in range(1, 4):
        acc = acc + pltpu.roll(ext, j, 0) * cw_ref[3 - j:4 - j, :]
    xc = acc.reshape(sb, SUBLANES + c, w)[:, SUBLANES:, :].reshape(sb * c, w)
    hist_sc[...] = u[:, c - SUBLANES:, :]

    xb = xc.astype(BF16)
    half = w // 2

    def gate(w_ref, b_ref):
        pre = jnp.concatenate([_dot(xb[:, :half], w_ref[0]), _dot(xb[:, half:], w_ref[1])], axis=1)
        return _sigmoid(pre + b_ref[...])

    r = gate(wr_ref, br_ref)
    i = gate(wi_ref, bi_ref)
    log_a = -LRU_C * r * _softplus(-lam_ref[...])
    a = jnp.exp(log_a)
    b = jnp.sqrt(1.0 - a * a) * (i * xc)

    t = _mod_pow2(lax.broadcasted_iota(jnp.int32, (sb * c, w), 0), c)
    s = 1
    while s < c:
        m = t >= s
        a_sh = pltpu.roll(a, s, 0)
        b_sh = pltpu.roll(b, s, 0)
        b = jnp.where(m, a * b_sh + b, b)
        a = jnp.where(m, a * a_sh, a)
        s *= 2

    h = a.reshape(sb, c, w) * h_sc[...] + b.reshape(sb, c, w)
    h_sc[...] = h[:, c - 1:c, :]
    ag = ag_ref[...].astype(F32)
    oa_ref[...] = (h * (ag * _sigmoid(ag))).astype(BF16)

    @pl.when(ci == pl.num_programs(1) - 1)
    def _():
        hlast_ref[...] = h_sc[...]


def _lru(ax, ag, cinit, h0, p, *, sb, c):
    n_seq, t, w = ax.shape
    kern = functools.partial(_lru_kernel, sb=sb, c=c)
    blk = lambda s: pl.BlockSpec(s, lambda i, j: (i, j, 0))
    return pl.pallas_call(
        kern,
        grid=(n_seq // sb, t // c),
        in_specs=[
            blk((sb, c, w)), blk((sb, c, w)),
            pl.BlockSpec((sb, SUBLANES, w), lambda i, j: (i, 0, 0)),
            pl.BlockSpec((sb, 1, w), lambda i, j: (i, 0, 0)),
            _const_spec((SUBLANES, w)), _const_spec((1, w)),
            _const_spec((2, w // 2, w // 2)), _const_spec((1, w)),
            _const_spec((2, w // 2, w // 2)), _const_spec((1, w)),
            _const_spec((1, w)),
        ],
        out_specs=[blk((sb, c, w)), pl.BlockSpec((sb, 1, w), lambda i, j: (i, 0, 0))],
        out_shape=[jax.ShapeDtypeStruct((n_seq, t, w), BF16), jax.ShapeDtypeStruct((n_seq, 1, w), F32)],
        scratch_shapes=[pltpu.VMEM((sb, SUBLANES, w), F32), pltpu.VMEM((sb, 1, w), F32)],
        compiler_params=_cparams(("arbitrary", "arbitrary")),
        name="rg_lru",
    )(ax, ag, cinit, h0, p["conv_w8"], p["conv_b"], p["lru_wr2"], p["lru_br"], p["lru_wi2"],
      p["lru_bi"], p["lru_lam"])


def _rwkv_kernel(bs_ref, bg_ref, pinit_ref, s0_ref, mu_ref, w0_ref, lora_ref, a0_ref, kkp_ref,
                 ka_ref, rk_ref, gng_ref, gnb_ref, bones_ref, ob_ref, sout_ref, hist_sc, s_sc,
                 *, sb, c):
    ci = pl.program_id(1)
    rows = sb * c
    hw = RW_WIDTH
    tb = RW_TB
    ntb = rows // tb
    spt = tb // c
    gt = RW_PAIRS * ntb
    gs = RW_PAIRS * sb

    @pl.when(ci == 0)
    def _():
        hist_sc[...] = pinit_ref[...]
        z = jnp.zeros((sb, RW_HD, RW_HD), F32)
        for p in range(RW_PAIRS):
            top = jnp.concatenate([s0_ref[:, 2 * p], z], axis=2)
            bot = jnp.concatenate([z, s0_ref[:, 2 * p + 1]], axis=2)
            s_sc[p] = jnp.concatenate([top, bot], axis=1)

    s3 = bs_ref[...]
    ext = jnp.concatenate([hist_sc[...], s3], axis=1).reshape(sb * (SUBLANES + c), SHIFT_COLS)
    sprev = pltpu.roll(ext, 1, 0).reshape(sb, SUBLANES + c, SHIFT_COLS)[:, SUBLANES:, :]
    sprev = sprev.reshape(rows, SHIFT_COLS)
    hist_sc[...] = s3[:, c - SUBLANES:, :]
    s = s3.reshape(rows, SHIFT_COLS)
    xs = s + (sprev - s) * mu_ref[...]
    r = xs[:, 0:hw]
    k = xs[:, hw:2 * hw]
    v = xs[:, 2 * hw:3 * hw]
    la = xs[:, 3 * hw:3 * hw + LANES]
    lane = lax.broadcasted_iota(jnp.int32, (rows, LANES), 1)
    lin = jnp.where(lane < RW_HD, jnp.tanh(la), la).astype(BF16)
    lo = _dot(lin, lora_ref[...])
    wlog = -_softplus(-(w0_ref[...] + lo[:, :hw])) - 0.5
    logw = -jnp.exp(wlog)
    a = _sigmoid(a0_ref[...] + lo[:, hw:])
    bones = bones_ref[...]
    kk = k * kkp_ref[...]
    kk = kk * lax.rsqrt(_dot_sel_r(kk * kk, bones) + 1e-12)
    kf = k * (1.0 + (a - 1.0) * ka_ref[...])
    b = kk * a

    ri = lax.broadcasted_iota(jnp.int32, (rows, rows), 0)
    cj = lax.broadcasted_iota(jnp.int32, (rows, rows), 1)
    same_seq = _div_pow2(ri, c) == _div_pow2(cj, c)
    ltri = jnp.where(same_seq & (cj <= ri), 1.0, 0.0).astype(BF16)
    lseq = jnp.where(same_seq, 1.0, 0.0).astype(BF16)
    log_g = _dot_sel_l3(ltri, logw)
    log_end = _dot_sel_l3(lseq, logw)
    g_inv = jnp.exp(-log_g)
    g_rem = jnp.exp(log_end - log_g)
    g_end = jnp.exp(log_end)

    low = lax.broadcasted_iota(jnp.int32, (1, 1, LANES), 2) < RW_HD

    def to_tb(x):
        return jnp.stack([x[tb * kb:tb * (kb + 1), LANES * p:LANES * (p + 1)]
                          for p in range(RW_PAIRS) for kb in range(ntb)], axis=0)

    def stack2(x3):
        return jnp.concatenate([jnp.where(low, x3, 0.0), jnp.where(low, 0.0, x3)], axis=1)

    def dup2(x3):
        return jnp.concatenate([x3, x3], axis=1)

    def to_ps(x3):
        if spt == 1:
            return x3
        return jnp.stack([jnp.concatenate([x3[g, c * q:c * (q + 1)], x3[g, tb + c * q:tb + c * (q + 1)]], axis=0)
                          for g in range(gt) for q in range(spt)], axis=0)

    def from_ps(y3):
        if spt == 1:
            return y3
        return jnp.stack([jnp.concatenate([y3[g * spt + q, :c] for q in range(spt)]
                                          + [y3[g * spt + q, c:] for q in range(spt)], axis=0)
                          for g in range(gt)], axis=0)

    xk = stack2(to_tb(kk * jnp.exp(log_g - logw)))
    xr = stack2(to_tb(r * jnp.exp(log_g)))
    vst = stack2(to_tb(v))
    kgs = stack2(to_tb(kf * g_rem))
    bgs = stack2(to_tb(b * g_rem))
    yb = dup2(to_tb(b * g_inv)).astype(BF16)
    yk = dup2(to_tb(kf * g_inv)).astype(BF16)
    xx = jnp.concatenate([xk, xr], axis=1).astype(BF16)
    ab = _bdot_nt(xx, yb)
    ak = _bdot_nt(xx, yk)

    n2 = 2 * tb
    i2 = lax.broadcasted_iota(jnp.int32, (1, n2, n2), 1)
    j2 = lax.broadcasted_iota(jnp.int32, (1, n2, n2), 2)
    same_blk = _div_pow2(i2, c) == _div_pow2(j2, c)
    m_strict = same_blk & (j2 < i2)
    m_incl = same_blk & (j2 <= i2)
    eye = jnp.where(i2 == j2, 1.0, 0.0)
    a_bk = jnp.where(m_strict, ab[:, :n2], 0.0)
    a_br = jnp.where(m_incl, ab[:, n2:], 0.0)
    a_kk = jnp.where(m_strict, ak[:, :n2], 0.0)
    a_kr = jnp.where(m_incl, ak[:, n2:], 0.0)

    pw = -a_bk
    tinv = eye + pw
    for _ in range(int(math.log2(c)) - 1):
        pw = _bdot16(pw, pw)
        tinv = tinv + _bdot16(tinv, pw)

    st = s_sc[...].reshape(gs, LANES, LANES)
    stb = st.astype(BF16)
    if spt == 1:
        res = _bdot_nt(xx, stb)
        xks, xrs = res[:, :n2], res[:, n2:]
    else:
        xks = from_ps(_bdot_nt(to_ps(xk).astype(BF16), stb))
        xrs = from_ps(_bdot_nt(to_ps(xr).astype(BF16), stb))
    vb = vst.astype(BF16)
    uu = _bdot16(tinv, xks + _bdot(a_kk.astype(BF16), vb))
    ost = xrs + _bdot(a_kr.astype(BF16), vb) - _bdot(a_br.astype(BF16), uu.astype(BF16))
    o3 = ost[:, :tb] + ost[:, tb:]

    lhs = jnp.concatenate([to_ps(vst), -to_ps(uu)], axis=1).astype(BF16)
    rhs = jnp.concatenate([to_ps(kgs), to_ps(bgs)], axis=1).astype(BF16)
    ds_ = _bdot_tn(lhs, rhs)
    gend = jnp.stack([g_end[c * q:c * q + 1, LANES * p:LANES * (p + 1)]
                      for p in range(RW_PAIRS) for q in range(sb)], axis=0)
    s_sc[...] = (st * gend + ds_).reshape(RW_PAIRS, sb, LANES, LANES)

    o = jnp.concatenate([jnp.concatenate([o3[p * ntb + kb] for kb in range(ntb)], axis=0)
                         for p in range(RW_PAIRS)], axis=1)
    inv_hd = 1.0 / RW_HD
    mean = _dot_sel_r(o, bones) * inv_hd
    d = o - mean
    var = _dot_sel_r(d * d, bones) * inv_hd
    on = d * lax.rsqrt(var + GN_EPS) * gng_ref[...] + gnb_ref[...]
    bonus = _dot_sel_r(r * kf * rk_ref[...], bones) * v
    g = bg_ref[...].astype(F32).reshape(rows, hw)
    ob_ref[...] = ((on + bonus) * (g * _sigmoid(g))).astype(BF16).reshape(sb, c, hw)

    @pl.when(ci == pl.num_programs(1) - 1)
    def _():
        for p in range(RW_PAIRS):
            sp = s_sc[p]
            sout_ref[:, 2 * p] = sp[:, :RW_HD, :RW_HD]
            sout_ref[:, 2 * p + 1] = sp[:, RW_HD:, RW_HD:]


def _rwkv(bs, bg, pinit, s0, p, *, sb, c):
    n_seq, t, _ = bs.shape
    hw = RW_WIDTH
    assert (sb * c) % RW_TB == 0 and RW_TB % c == 0
    kern = functools.partial(_rwkv_kernel, sb=sb, c=c)
    blk = lambda s: pl.BlockSpec(s, lambda i, j: (i, j, 0))
    st_spec = pl.BlockSpec((sb, 2 * RW_PAIRS, RW_HD, RW_HD), lambda i, j: (i, 0, 0, 0))
    return pl.pallas_call(
        kern,
        grid=(n_seq // sb, t // c),
        in_specs=[
            blk((sb, c, SHIFT_COLS)), blk((sb, c, hw)),
            pl.BlockSpec((sb, SUBLANES, SHIFT_COLS), lambda i, j: (i, 0, 0)),
            st_spec,
            _const_spec((1, SHIFT_COLS)), _const_spec((1, hw)), _const_spec((LANES, 2 * hw)),
            _const_spec((1, hw)), _const_spec((1, hw)), _const_spec((1, hw)), _const_spec((1, hw)),
            _const_spec((1, hw)), _const_spec((1, hw)), _const_spec((hw, hw)),
        ],
        out_specs=[blk((sb, c, hw)), st_spec],
        out_shape=[jax.ShapeDtypeStruct((n_seq, t, hw), BF16),
                   jax.ShapeDtypeStruct((n_seq, 2 * RW_PAIRS, RW_HD, RW_HD), F32)],
        scratch_shapes=[pltpu.VMEM((sb, SUBLANES, SHIFT_COLS), F32),
                        pltpu.VMEM((RW_PAIRS, sb, LANES, LANES), F32)],
        compiler_params=_cparams(("arbitrary", "arbitrary")),
        name="rwkv7",
    )(bs, bg, pinit, s0, p["rw_mu"], p["rw_w0"], p["rw_lora"], p["rw_a0"], p["rw_kk"], p["rw_ka"],
      p["rw_rk"], p["rw_gn_g"], p["rw_gn_b"], p["bones"])


def _rope128(x, cos, sin_a, sin_b):
    return x * cos + pltpu.roll(x, LANES - ROPE_DIM // 2, 1) * sin_a + pltpu.roll(x, ROPE_DIM // 2, 1) * sin_b


def _mla_prep_kernel(cq_ref, ckv_ref, pe_ref, cos_ref, sa_ref, sb_ref, gqa_ref, wq_ref, gq_ref,
                     gkva_ref, wk_ref, gk_ref, wvt_ref, ckvn_ref, q_ref, *kv_refs, sb, tt):
    rows = sb * tt
    cq = cq_ref[...].reshape(rows, Q_RANK)
    cqn = (cq * lax.rsqrt(jnp.mean(cq * cq, axis=-1, keepdims=True) + EPS) * gqa_ref[...]).astype(BF16)
    q = _dot(cqn, wq_ref[...])
    ckv = ckv_ref[...].reshape(rows, KV_RANK)
    ckvn = ckv * lax.rsqrt(jnp.mean(ckv * ckv, axis=-1, keepdims=True) + EPS) * gkva_ref[...]
    ckvn_ref[...] = ckvn.reshape(sb, tt, KV_RANK)

    def tile(ref):
        return jnp.broadcast_to(ref[...][None], (sb, tt, LANES)).reshape(rows, LANES)

    cos, sa, sb_ = tile(cos_ref), tile(sa_ref), tile(sb_ref)
    inv_d = 1.0 / QK_DIM
    for h in range(MLA_HEADS):
        qh = q[:, h * LANES:(h + 1) * LANES]
        qn = qh * lax.rsqrt(jnp.sum(qh * qh, axis=-1, keepdims=True) * inv_d + EPS) * gq_ref[...]
        q_ref[:, h] = (_rope128(qn, cos, sa, sb_) * SCALE).astype(BF16).reshape(sb, tt, LANES)
    if not kv_refs:
        return
    k_ref, vt_ref = kv_refs
    cb = ckvn.astype(BF16)
    kvk = _dot(cb, wk_ref[...])
    vt = _dot_nt(wvt_ref[...], cb)
    pe = pe_ref[...].reshape(rows, LANES)
    extra = lax.broadcasted_iota(jnp.int32, (VT_ROWS - V_DIM, rows), 0)
    ones_rows = jnp.where(extra == 0, 1.0, 0.0)
    for h in range(MLA_HEADS):
        kh = kvk[:, h * LANES:(h + 1) * LANES] + pe
        kn = kh * lax.rsqrt(jnp.sum(kh * kh, axis=-1, keepdims=True) * inv_d + EPS) * gk_ref[...]
        k_ref[:, h] = _rope128(kn, cos, sa, sb_).astype(BF16).reshape(sb, tt, LANES)
        vt_ref[0, h] = jnp.concatenate([vt[h * V_DIM:(h + 1) * V_DIM], ones_rows], axis=0).astype(BF16)


def _mla_prep(cq, ckv, pe_a, tabs, p, *, sb, tt, with_kv):
    n_seq, t, _ = cq.shape
    assert sb == 1 or not with_kv
    kern = functools.partial(_mla_prep_kernel, sb=sb, tt=tt)
    blk = lambda w: pl.BlockSpec((sb, tt, w), lambda i, j: (i, j, 0))
    tab = pl.BlockSpec((tt, LANES), lambda i, j: (j, 0))
    hd = pl.BlockSpec((sb, MLA_HEADS, tt, LANES), lambda i, j: (i, 0, j, 0))
    hshape = jax.ShapeDtypeStruct((n_seq, MLA_HEADS, t, LANES), BF16)
    out_specs = [blk(KV_RANK), hd]
    out_shape = [jax.ShapeDtypeStruct((n_seq, t, KV_RANK), F32), hshape]
    if with_kv:
        out_specs += [hd, pl.BlockSpec((sb, MLA_HEADS, VT_ROWS, tt), lambda i, j: (i, 0, 0, j))]
        out_shape += [hshape, jax.ShapeDtypeStruct((n_seq, MLA_HEADS, VT_ROWS, t), BF16)]
    return pl.pallas_call(
        kern,
        grid=(n_seq // sb, t // tt),
        in_specs=[
            blk(Q_RANK), blk(KV_RANK), blk(LANES), tab, tab, tab,
            _const_spec((1, Q_RANK)), _const_spec((Q_RANK, MLA_HEADS * LANES)), _const_spec((1, LANES)),
            _const_spec((1, KV_RANK)), _const_spec((KV_RANK, MLA_HEADS * LANES)), _const_spec((1, LANES)),
            _const_spec((MLA_WIDTH, KV_RANK)),
        ],
        out_specs=out_specs,
        out_shape=out_shape,
        compiler_params=_cparams(("arbitrary", "arbitrary")),
        name="mla_prep",
    )(cq, ckv, pe_a, tabs[0], tabs[1], tabs[2], p["g_qa"], p["wq"], p["gq128"], p["g_kva"],
      p["wk"], p["gk128"], p["wvt"])


def _flash_kernel(q_ref, k_ref, vt_ref, o_ref, *, tq, tk):
    qi = pl.program_id(2)
    nsub = tq // tk
    qs = [q_ref[0, hh] for hh in range(2)]

    def tile(hh, j, carry, q0, mask):
        m, acc = carry
        kj = k_ref[0, hh, pl.ds(j * tk, tk), :]
        vj = vt_ref[0, hh, :, pl.ds(j * tk, tk)].astype(F32)
        s = _dot_nt(kj, qs[hh][q0:])
        if mask is not None:
            s = jnp.where(mask, s, NEG)
        m_old = m[:, q0:]
        m_new = jnp.maximum(m_old, jnp.max(s, axis=0, keepdims=True))
        pr = jnp.exp(s - m_new)
        acc_new = acc[:, q0:] * jnp.exp(m_old - m_new) + _dot(vj, pr)
        if q0:
            m_new = jnp.concatenate([m[:, :q0], m_new], axis=1)
            acc_new = jnp.concatenate([acc[:, :q0], acc_new], axis=1)
        return m_new, acc_new

    def step(j, carry):
        return tuple(tile(hh, j, carry[hh], 0, None) for hh in range(2))

    init1 = (jnp.full((1, tq), NEG, F32), jnp.zeros((VT_ROWS, tq), F32))
    carry = lax.fori_loop(0, qi * nsub, step, (init1, init1))
    for d in range(nsub):
        q0 = d * tk
        krow = lax.broadcasted_iota(jnp.int32, (tk, tq - q0), 0)
        qcol = lax.broadcasted_iota(jnp.int32, (tk, tq - q0), 1)
        carry = tuple(tile(hh, qi * nsub + d, carry[hh], q0, krow <= qcol) for hh in range(2))
    ot = jnp.concatenate([carry[hh][1][:V_DIM] / carry[hh][1][V_DIM:V_DIM + 1] for hh in range(2)],
                         axis=0)
    o_ref[0] = ot.T.astype(BF16)


def _flash(q, k, vt, *, tq, tk):
    b, h, t, _ = q.shape
    kern = functools.partial(_flash_kernel, tq=tq, tk=tk)
    return pl.pallas_call(
        kern,
        grid=(b, h // 2, t // tq),
        in_specs=[
            pl.BlockSpec((1, 2, tq, LANES), lambda bi, pi, qi: (bi, pi, qi, 0)),
            pl.BlockSpec((1, 2, t, LANES), lambda bi, pi, qi: (bi, pi, 0, 0)),
            pl.BlockSpec((1, 2, VT_ROWS, t), lambda bi, pi, qi: (bi, pi, 0, 0)),
        ],
        out_specs=pl.BlockSpec((1, tq, LANES), lambda bi, pi, qi: (bi, qi, pi)),
        out_shape=jax.ShapeDtypeStruct((b, t, (h // 2) * LANES), BF16),
        compiler_params=_cparams(("arbitrary", "arbitrary", "arbitrary")),
        name="mla_flash",
    )(q, k, vt)


def _decode_kernel(pt_ref, q_ref, cnew_ref, penew_ref, ckv_hbm, kpe_hbm, wukt_ref, gkn_ref, sel_ref,
                   wukr_ref, gkr_ref, gkrt_ref, ct_ref, st_ref, tn_ref, wuv_ref, o_ref,
                   cbuf, pbuf, s_sc, cb_sc, sem, *, layer, pc, n_chunks, ts):
    b = pl.program_id(0)
    nb = pl.num_programs(0)
    page = ckv_hbm.shape[2]
    hq = MLA_HEADS * ts
    half = ROPE_DIM // 2

    def copies(bb, ch, slot):
        out = []
        for i in range(pc):
            phys = pt_ref[bb, ch * pc + i]
            out.append(pltpu.make_async_copy(
                ckv_hbm.at[layer, phys], cbuf.at[slot, pl.ds(i * page, page), :], sem.at[0, slot]))
            out.append(pltpu.make_async_copy(
                kpe_hbm.at[layer, phys], pbuf.at[slot, :, pl.ds(i * page, page)], sem.at[1, slot]))
        return out

    @pl.when(b == 0)
    def _():
        for cp in copies(0, 0, 0):
            cp.start()

    qf = q_ref[0].astype(F32).reshape(hq, LANES)
    qg = qf * gkn_ref[...]
    qlat = jnp.concatenate(
        [_dot(qg[h * ts:(h + 1) * ts].astype(BF16), wukt_ref[h]) for h in range(MLA_HEADS)],
        axis=0).astype(BF16)
    qrope = _dot(qf.astype(BF16), sel_ref[...])
    qrope32 = qrope[:, :ROPE_DIM].astype(BF16)
    qrope128 = qrope.astype(BF16)
    inv_d = 1.0 / QK_DIM
    n_nope = MLA_HEADS * NOPE_DIM
    lhs_all = jnp.concatenate([wukr_ref[...], qlat], axis=0)

    def latent_scores(cb):
        res = _dot_nt(lhs_all, cb)
        rows = []
        for h in range(MLA_HEADS):
            kt = res[h * NOPE_DIM:(h + 1) * NOPE_DIM]
            ssh = jnp.sum(kt * kt, axis=0, keepdims=True)
            rows.append(jnp.broadcast_to(ssh, (ts, ssh.shape[1])))
        return res[n_nope:], jnp.concatenate(rows, axis=0)

    def update(carry, s, cb):
        m, l, acc = carry
        m_new = jnp.maximum(m, jnp.max(s, axis=-1, keepdims=True))
        pr = jnp.exp(s - m_new)
        corr = jnp.exp(m - m_new)
        l = l * corr + jnp.sum(pr, axis=-1, keepdims=True)
        acc = acc * corr + _dot(pr.astype(BF16), cb)
        return m_new, l, acc

    def dma_step(ch):
        slot = (b * n_chunks + ch) % 2
        for cp in copies(b, ch, slot):
            cp.wait()

        @pl.when(ch + 1 < n_chunks)
        def _():
            for cp in copies(b, ch + 1, 1 - slot):
                cp.start()

        @pl.when((ch + 1 == n_chunks) & (b + 1 < nb))
        def _():
            for cp in copies(b + 1, 0, 1 - slot):
                cp.start()
        return slot

    def stage_a(ch, slot, par):
        krs, pss = [], []
        for i in range(pc):
            pg = ch * pc + i
            x = pbuf[slot, :, i * page:(i + 1) * page]
            xg = x * gkrt_ref[...]
            sw = jnp.concatenate([xg[half:], xg[:half]], axis=0)
            krs.append((xg * ct_ref[pg] + sw * st_ref[pg]).astype(BF16))
            pss.append(jnp.sum(x * x, axis=0, keepdims=True))
        cb = cbuf[slot].astype(BF16)
        kr = jnp.concatenate(krs, axis=1)
        pesq = jnp.concatenate(pss, axis=1)
        s_lat, ssn = latent_scores(cb)
        s = s_lat + _dot(qrope32, kr)
        s_sc[par] = s * lax.rsqrt((ssn + pesq) * inv_d + EPS)
        cb_sc[par] = cb

    def step(ch, par, carry):
        slot = dma_step(ch)
        stage_a(ch, slot, par)
        return update(carry, s_sc[1 - par], cb_sc[1 - par])

    init = (jnp.full((hq, 1), NEG, F32), jnp.zeros((hq, 1), F32), jnp.zeros((hq, LANES), F32))
    stage_a(0, dma_step(0), 0)
    n_steps = n_chunks - 1

    def pair(j, carry):
        carry = step(2 * j + 1, 1, carry)
        return step(2 * j + 2, 0, carry)

    carry = lax.fori_loop(0, n_steps // 2, pair, init)
    if n_steps % 2:
        carry = step(n_chunks - 1, 1, carry)
    last = (n_chunks - 1) % 2
    carry = update(carry, s_sc[last], cb_sc[last])

    pad = jnp.zeros((NEW_ROWS - ts, LANES), F32)
    cn = jnp.concatenate([cnew_ref[0], pad], axis=0)
    pn = jnp.concatenate([penew_ref[0], pad], axis=0)
    krn = _rope128(pn * gkr_ref[...], tn_ref[0], tn_ref[1], tn_ref[2]).astype(BF16)
    cnb = cn.astype(BF16)
    ones_r = jnp.ones((hq, LANES), BF16)
    s_lat, ssn = latent_scores(cnb)
    ssq = ssn + _dot_nt(ones_r, (pn * pn).astype(BF16))
    s = (s_lat + _dot_nt(qrope128, krn)) * lax.rsqrt(ssq * inv_d + EPS)
    qidx = _mod_pow2(lax.broadcasted_iota(jnp.int32, (hq, NEW_ROWS), 0), ts)
    tidx = lax.broadcasted_iota(jnp.int32, (hq, NEW_ROWS), 1)
    s = jnp.where(tidx <= qidx, s, NEG)
    _, l, acc = update(carry, s, cnb)
    al = acc / l
    out = _dot(al[0:ts].astype(BF16), wuv_ref[0])
    for h in range(1, MLA_HEADS):
        out = out + _dot(al[h * ts:(h + 1) * ts].astype(BF16), wuv_ref[h])
    o_ref[0] = out


def _decode(page_table, q, cnew, penew, cache_ckv, cache_kpe_t, tabs_c, tabs_s, tabs_n, p, *, layer, pc):
    db, h, ts, _ = q.shape
    n_pages = page_table.shape[1]
    page = cache_ckv.shape[2]
    assert n_pages % pc == 0
    n_chunks = n_pages // pc
    tok = pc * page
    kern = functools.partial(_decode_kernel, layer=layer, pc=pc, n_chunks=n_chunks, ts=ts)
    cs = lambda shape: pl.BlockSpec(shape, lambda b, pt: (0,) * len(shape))
    cs1 = lambda shape: pl.BlockSpec(shape, lambda b, pt: (0,) * len(shape), pipeline_mode=pl.Buffered(1))
    grid_spec = pltpu.PrefetchScalarGridSpec(
        num_scalar_prefetch=1,
        grid=(db,),
        in_specs=[
            pl.BlockSpec((1, h, ts, LANES), lambda b, pt: (b, 0, 0, 0)),
            pl.BlockSpec((1, ts, LANES), lambda b, pt: (b, 0, 0)),
            pl.BlockSpec((1, ts, LANES), lambda b, pt: (b, 0, 0)),
            pl.BlockSpec(memory_space=pl.ANY),
            pl.BlockSpec(memory_space=pl.ANY),
            cs((h, LANES, LANES)), cs((1, LANES)), cs((LANES, LANES)),
            cs((h * NOPE_DIM, KV_RANK)), cs((1, LANES)), cs((ROPE_DIM, LANES)),
            cs1((n_pages, ROPE_DIM, page)), cs1((n_pages, ROPE_DIM, page)), cs((3, NEW_ROWS, LANES)),
            cs((h, KV_RANK, MLA_WIDTH)),
        ],
        out_specs=pl.BlockSpec((1, ts, MLA_WIDTH), lambda b, pt: (b, 0, 0)),
        scratch_shapes=[
            pltpu.VMEM((2, tok, KV_RANK), F32),
            pltpu.VMEM((2, ROPE_DIM, tok), F32),
            pltpu.VMEM((2, h * ts, tok), F32),
            pltpu.VMEM((2, tok, KV_RANK), BF16),
            pltpu.SemaphoreType.DMA((2, 2)),
        ],
    )
    return pl.pallas_call(
        kern,
        grid_spec=grid_spec,
        out_shape=jax.ShapeDtypeStruct((db, ts, MLA_WIDTH), F32),
        compiler_params=_cparams(("arbitrary",)),
        name="mla_decode",
    )(page_table, q, cnew, penew, cache_ckv, cache_kpe_t, p["wukt"], p["gkn128"], p["sel_rope"],
      p["wukr"], p["gkr128"], p["gkrt"], tabs_c, tabs_s, tabs_n, p["wuv"])


def _merge_kernel(x_ref, gate_ref, oa_ref, ob_ref, oc_ref, cg_ref, woa_ref, wob_ref, woc_ref,
                  wout_ref, y_ref):
    d = x_ref.shape[1]
    cg = cg_ref[...].astype(F32)
    oc = (oc_ref[...].astype(F32) * (cg * _sigmoid(cg))).astype(BF16)
    g = _sigmoid(gate_ref[...].astype(F32))
    merged = (g[:, 0:d] * _dot(oa_ref[...], woa_ref[...])
              + g[:, d:2 * d] * _dot(ob_ref[...], wob_ref[...])
              + g[:, 2 * d:3 * d] * _dot(oc, woc_ref[...]))
    y_ref[...] = x_ref[...] + _dot(merged.astype(BF16), wout_ref[...])


def _merge(x2d, gate, oa, ob, oc, cg, p):
    n, d = x2d.shape
    tm = min(512, n)
    row = lambda w: pl.BlockSpec((tm, w), lambda i: (i, 0))
    return pl.pallas_call(
        _merge_kernel,
        grid=(n // tm,),
        in_specs=[row(d), row(3 * d), row(LRU_WIDTH), row(RW_WIDTH), row(MLA_WIDTH), row(MLA_WIDTH),
                  _const_spec((LRU_WIDTH, d)), _const_spec((RW_WIDTH, d)), _const_spec((MLA_WIDTH, d)),
                  _const_spec((d, d))],
        out_specs=row(d),
        out_shape=jax.ShapeDtypeStruct((n, d), F32),
        compiler_params=_cparams(("arbitrary",)),
        name="merge",
    )(x2d, gate, oa, ob, oc, cg, p["w_oa"], p["w_ob"], p["w_oc"], p["w_out"])


def _rope_angles(pos):
    inv = 1.0 / (ROPE_THETA ** (jnp.arange(0, ROPE_DIM, 2, dtype=F32) / ROPE_DIM))
    ang = pos.astype(F32)[:, None] * inv[None, :]
    return jnp.cos(ang), jnp.sin(ang)


def _rope_pattern(pos, lane0, rows=None):
    half = ROPE_DIM // 2
    cos, sin = _rope_angles(pos)
    n = pos.shape[0]
    ones_before = jnp.ones((n, lane0), F32) if lane0 else jnp.zeros((n, 0), F32)
    z = lambda w: jnp.zeros((n, w), F32)
    rest = LANES - lane0 - ROPE_DIM
    cos_t = jnp.concatenate([ones_before, cos, cos, z(rest)], axis=1)
    sin_a = jnp.concatenate([z(lane0), -sin, z(half), z(rest)], axis=1)
    sin_b = jnp.concatenate([z(lane0), z(half), sin, z(rest)], axis=1)
    tabs = jnp.stack([cos_t, sin_a, sin_b], axis=0)
    if rows is not None and rows > n:
        tabs = jnp.pad(tabs, ((0, 0), (0, rows - n), (0, 0)))
    return tabs


def _rope_pages(n_pages, page):
    cos, sin = _rope_angles(jnp.arange(n_pages * page))
    ct = jnp.concatenate([cos, cos], axis=1)
    st = jnp.concatenate([-sin, sin], axis=1)
    lay = lambda x: jnp.transpose(x.reshape(n_pages, page, ROPE_DIM), (0, 2, 1))
    return lay(ct), lay(st)


def _block_diag2(w):
    nb, bs, _ = w.shape
    per = nb // 2
    out = jnp.zeros((2, per * bs, per * bs), w.dtype)
    for i in range(nb):
        hf, j = divmod(i, per)
        out = out.at[hf, j * bs:(j + 1) * bs, j * bs:(j + 1) * bs].set(w[i])
    return out


def _layer_params(l, ts, a):
    d = a["w_in"].shape[1]
    w = a["w_in"][l]
    src_pe = SEG_CKV[1]
    z = lambda n: jnp.zeros((d, n), w.dtype)
    w_pad = jnp.concatenate([
        w[:, :src_pe],
        z(64), w[:, src_pe:src_pe + ROPE_DIM], z(32),
        w[:, src_pe:src_pe + ROPE_DIM], z(96),
        w[:, src_pe + ROPE_DIM:],
    ], axis=1).astype(BF16)
    assert w_pad.shape[1] == IN_COLS_PAD
    row = lambda v: v.reshape(1, -1)
    hd = RW_HD
    bones = jnp.kron(jnp.eye(RW_WIDTH // hd, dtype=F32), jnp.ones((hd, hd), F32)).astype(BF16)
    zl = jnp.zeros((LORA, RW_WIDTH), F32)
    lora = jnp.concatenate([jnp.concatenate([a["rw_w2"][l], zl], axis=1),
                            jnp.concatenate([zl, a["rw_a2"][l]], axis=1)], axis=0).astype(BF16)
    w_uq = a["mla_w_uq"][l]
    w_ukv = a["mla_w_ukv"][l]
    w_uk, w_uv = w_ukv[..., :NOPE_DIM], w_ukv[..., NOPE_DIM:]
    g_kn = a["mla_g_kn"][l]
    pad_h = lambda x: jnp.pad(x, ((0, 0), (0, 0), (0, LANES - x.shape[-1]))).reshape(x.shape[0], -1)
    wukt = jnp.pad(jnp.transpose(w_uk, (1, 2, 0)), ((0, 0), (0, LANES - NOPE_DIM), (0, 0)))
    wukr = jnp.transpose(w_uk.reshape(KV_RANK, MLA_HEADS * NOPE_DIM))
    wuv = jnp.zeros((MLA_HEADS, KV_RANK, MLA_WIDTH), F32)
    for h in range(MLA_HEADS):
        wuv = wuv.at[h, :, h * V_DIM:(h + 1) * V_DIM].set(w_uv[:, h, :])
    sel = jnp.zeros((LANES, LANES), F32).at[
        jnp.arange(NOPE_DIM, QK_DIM), jnp.arange(ROPE_DIM)].set(1.0)
    return {
        "norm_g": row(a["norm_g"][l]), "w_in": w_pad,
        "conv_w8": jnp.pad(a["conv_w"][l], ((0, SUBLANES - a["conv_w"].shape[1]), (0, 0))),
        "conv_b": row(a["conv_b"][l]),
        "lru_wr2": _block_diag2(a["lru_wr"][l]).astype(BF16), "lru_br": row(a["lru_br"][l]),
        "lru_wi2": _block_diag2(a["lru_wi"][l]).astype(BF16), "lru_bi": row(a["lru_bi"][l]),
        "lru_lam": row(a["lru_lam"][l]),
        "rw_mu": row(a["rw_mu"][l]), "rw_w0": row(a["rw_w0"][l]), "rw_lora": lora,
        "rw_a0": row(a["rw_a0"][l]), "rw_kk": row(a["rw_kk"][l]), "rw_ka": row(a["rw_ka"][l]),
        "rw_rk": row(a["rw_rk"][l]), "rw_gn_g": row(a["rw_gn_g"][l]), "rw_gn_b": row(a["rw_gn_b"][l]),
        "bones": bones,
        "g_qa": row(a["mla_g_qa"][l]), "wq": pad_h(w_uq).astype(BF16),
        "gq128": row(jnp.pad(a["mla_g_qn"][l], (0, LANES - QK_DIM))),
        "g_kva": row(a["mla_g_kva"][l]), "wk": pad_h(w_uk).astype(BF16),
        "gk128": row(jnp.pad(g_kn, (0, LANES - QK_DIM))),
        "wvt": jnp.transpose(w_uv.reshape(KV_RANK, MLA_WIDTH)).astype(BF16),
        "wukt": wukt.astype(BF16),
        "gkn128": row(jnp.pad(g_kn[:NOPE_DIM], (0, LANES - NOPE_DIM))),
        "sel_rope": sel.astype(BF16), "wukr": wukr.astype(BF16),
        "gkr128": row(jnp.pad(g_kn[NOPE_DIM:], (0, LANES - ROPE_DIM))),
        "gkrt": jnp.broadcast_to(g_kn[NOPE_DIM:, None], (ROPE_DIM, LANES)),
        "wuv": wuv.astype(BF16),
        "w_oa": a["w_oa"][l].astype(BF16), "w_ob": a["w_ob"][l].astype(BF16),
        "w_oc": a["w_oc"][l].astype(BF16), "w_out": a["w_out"][l].astype(BF16),
    }


def _hist_rows(x):
    return jnp.pad(x, ((0, 0), (SUBLANES - x.shape[1], 0), (0, 0)))


def _mixer_layer(x, p, lru_h0, conv_buf, rw_s0, rw_prev, tabs, attend, *, lru_blk, rw_blk, prep_blk,
                 with_kv):
    n_seq, t, d = x.shape
    n = n_seq * t
    segs = _inproj(x.reshape(n, d), p["norm_g"], p["w_in"])
    ax, ag, bs, bg, cq, ckv_raw, pe_a, pe_b, cg, gate = segs
    r3 = lambda v: v.reshape(n_seq, t, v.shape[-1])
    oa, lru_h = _lru(r3(ax), r3(ag), _hist_rows(conv_buf), lru_h0[:, None, :], p, sb=lru_blk[0],
                     c=lru_blk[1])
    lru_h = lru_h[:, 0, :]
    ob, rw_s = _rwkv(r3(bs), r3(bg), _hist_rows(rw_prev[:, None, :]), rw_s0, p, sb=rw_blk[0], c=rw_blk[1])
    prep = _mla_prep(r3(cq), r3(ckv_raw), r3(pe_a), tabs, p, sb=prep_blk[0], tt=prep_blk[1],
                     with_kv=with_kv)
    ckv = prep[0]
    oc = attend(prep, r3(pe_b))
    y = _merge(x.reshape(n, d), gate, oa.reshape(n, -1), ob.reshape(n, -1), oc.reshape(n, -1), cg, p)
    conv_new = r3(ax)[:, t - conv_buf.shape[1]:, :]
    shift_new = r3(bs)[:, t - 1, :]
    kpe = r3(pe_b)[:, :, :ROPE_DIM]
    return y.reshape(n_seq, t, d), (lru_h, conv_new, rw_s, shift_new, ckv, kpe)


def kernel(x_prompt, x_sample, state_lru_h, state_lru_conv, state_rwkv_S, state_rwkv_shift, cache_ckv, cache_kpe, page_table, norm_g, w_in, conv_w, conv_b, lru_wr, lru_br, lru_wi, lru_bi, lru_lam, rw_mu, rw_w0, rw_w2, rw_a0, rw_a2, rw_kk, rw_ka, rw_rk, rw_gn_g, rw_gn_b, mla_g_qa, mla_g_kva, mla_w_uq, mla_w_ukv, mla_g_qn, mla_g_kn, w_oa, w_ob, w_oc, w_out):
    a = dict(norm_g=norm_g, w_in=w_in, conv_w=conv_w, conv_b=conv_b, lru_wr=lru_wr, lru_br=lru_br,
             lru_wi=lru_wi, lru_bi=lru_bi, lru_lam=lru_lam, rw_mu=rw_mu, rw_w0=rw_w0, rw_w2=rw_w2,
             rw_a0=rw_a0, rw_a2=rw_a2, rw_kk=rw_kk, rw_ka=rw_ka, rw_rk=rw_rk, rw_gn_g=rw_gn_g,
             rw_gn_b=rw_gn_b, mla_g_qa=mla_g_qa, mla_g_kva=mla_g_kva, mla_w_uq=mla_w_uq,
             mla_w_ukv=mla_w_ukv, mla_g_qn=mla_g_qn, mla_g_kn=mla_g_kn, w_oa=w_oa, w_ob=w_ob,
             w_oc=w_oc, w_out=w_out)
    bsz, t, _ = x_prompt.shape
    db, ts, _ = x_sample.shape
    depth = w_in.shape[0]
    n_pages = page_table.shape[1]
    page = cache_ckv.shape[2]
    past = n_pages * page
    dt = x_prompt.dtype
    assert ts == SUBLANES and t % RW_TB == 0 and db % SUBLANES == 0

    tabs_p = _rope_pattern(jnp.arange(t), NOPE_DIM)
    tabs_s = _rope_pattern(past + jnp.arange(ts), NOPE_DIM)
    dec_c, dec_s = _rope_pages(n_pages, page)
    dec_n = _rope_pattern(past + jnp.arange(ts), 0, rows=NEW_ROWS)
    cache_kpe_t = jnp.transpose(cache_kpe, (0, 1, 3, 2))

    tq = min(FLASH_TQ, t)
    tk = min(FLASH_TK, t)
    pc = min(DECODE_PAGES, n_pages)
    lru_c = min(256, t)
    prep_tt = min(512, t)
    sb_s = min(32, db)

    hp, hs = x_prompt, x_sample
    st_p, st_s = [], []
    for l in range(depth):
        p = _layer_params(l, ts, a)

        def attend_prompt(prep, pe_b):
            _, q, k, vt = prep
            return _flash(q, k, vt, tq=tq, tk=tk)

        def attend_sample(prep, pe_b, l=l, p=p):
            ckv, q = prep
            return _decode(page_table, q, ckv, pe_b, cache_ckv, cache_kpe_t, dec_c, dec_s, dec_n, p,
                           layer=l, pc=pc)

        hp, sp = _mixer_layer(
            hp, p, jnp.zeros((bsz, LRU_WIDTH), dt), jnp.zeros((bsz, 3, LRU_WIDTH), dt),
            jnp.zeros((bsz, 2 * RW_PAIRS, RW_HD, RW_HD), dt),
            jnp.zeros((bsz, SHIFT_COLS), dt), tabs_p, attend_prompt, lru_blk=(bsz, lru_c),
            rw_blk=(bsz, RW_TB), prep_blk=(1, prep_tt), with_kv=True)
        hs, ss = _mixer_layer(
            hs, p, state_lru_h[l], state_lru_conv[l], state_rwkv_S[l], state_rwkv_shift[l],
            tabs_s, attend_sample, lru_blk=(sb_s, ts), rw_blk=(RW_TB // ts, ts),
            prep_blk=(sb_s, ts), with_kv=False)
        st_p.append(sp)
        st_s.append(ss)

    stack = lambda states, i: jnp.stack([s[i] for s in states], axis=0)
    return (hp, hs) + tuple(stack(st_p, i) for i in range(6)) + tuple(stack(st_s, i) for i in range(6))
```

```python
import functools
import math

import jax
import jax.numpy as jnp
from jax import lax
from jax.experimental import pallas as pl
from jax.experimental.pallas import tpu as pltpu

F32 = jnp.float32
BF16 = jnp.bfloat16

EPS = 1e-6
LRU_C = 8.0
GN_EPS = 64e-5
ROPE_THETA = 10000.0
NEG = -1e30

LANES = 128
SUBLANES = 8
VMEM_LIMIT_BYTES = 56 * 1024 * 1024

LRU_WIDTH = 512
RW_WIDTH = 512
RW_HD = 64
RW_PAIRS = RW_WIDTH // LANES
LORA = 64
SHIFT_COLS = 3 * RW_WIDTH + 2 * LORA
Q_RANK = 256
KV_RANK = 128
MLA_HEADS = 8
NOPE_DIM = 64
ROPE_DIM = 32
V_DIM = 64
QK_DIM = NOPE_DIM + ROPE_DIM
MLA_WIDTH = MLA_HEADS * V_DIM
SCALE = QK_DIM ** -0.5

RW_TB = 64
FLASH_TQ = 2048
FLASH_TK = 1024
DECODE_PAGES = 32
NEW_ROWS = 16
VT_ROWS = 80

SEG_AX = (0, 512)
SEG_AG = (512, 1024)
SEG_BS = (1024, 2688)
SEG_BG = (2688, 3200)
SEG_CQ = (3200, 3456)
SEG_CKV = (3456, 3584)
SEG_PEA = (3584, 3712)
SEG_PEB = (3712, 3840)
SEG_CG = (3840, 4352)
SEG_GATE = (4352, 7424)
IN_SEGS = (SEG_AX, SEG_AG, SEG_BS, SEG_BG, SEG_CQ, SEG_CKV, SEG_PEA, SEG_PEB, SEG_CG, SEG_GATE)
IN_DTYPES = (F32, BF16, F32, BF16, F32, F32, F32, F32, BF16, BF16)
IN_COLS_PAD = SEG_GATE[1]


def _cparams(sem):
    return pltpu.CompilerParams(dimension_semantics=sem, vmem_limit_bytes=VMEM_LIMIT_BYTES)


def _const_spec(shape):
    nd = len(shape)
    return pl.BlockSpec(shape, lambda *_: (0,) * nd)


def _sigmoid(x):
    return jax.nn.sigmoid(x)


def _softplus(x):
    return jnp.maximum(x, 0.0) + jnp.log(1.0 + jnp.exp(-jnp.abs(x)))


def _mod_pow2(x, n):
    return jnp.bitwise_and(x, n - 1)


def _div_pow2(x, n):
    return lax.shift_right_logical(x, int(math.log2(n)))


def _dot(a, b):
    return jnp.dot(a, b, preferred_element_type=F32)


def _dot_nt(a, b):
    return lax.dot_general(a, b, (((1,), (1,)), ((), ())), preferred_element_type=F32)


def _bdot(a, b):
    return jnp.einsum('gij,gjk->gik', a, b, preferred_element_type=F32)


def _bdot_nt(a, b):
    return jnp.einsum('gik,gjk->gij', a, b, preferred_element_type=F32)


def _bdot_tn(a, b):
    return jnp.einsum('gti,gtj->gij', a, b, preferred_element_type=F32)


def _split2(x):
    hi = x.astype(BF16)
    lo = (x - hi.astype(F32)).astype(BF16)
    return hi, lo


def _split3(x):
    hi = x.astype(BF16)
    r1 = x - hi.astype(F32)
    mid = r1.astype(BF16)
    lo = (r1 - mid.astype(F32)).astype(BF16)
    return hi, mid, lo


def _dot_sel_r(x, m01):
    hi, lo = _split2(x)
    return _dot(hi, m01) + _dot(lo, m01)


def _dot_sel_l3(m01, x):
    hi, mid, lo = _split3(x)
    return _dot(m01, hi) + _dot(m01, mid) + _dot(m01, lo)


def _bdot16(a, b):
    return _bdot(a.astype(BF16), b.astype(BF16))


def _inproj_kernel(x_ref, g_ref, w_ref, *out_refs):
    x = x_ref[...]
    ms = jnp.mean(x * x, axis=-1, keepdims=True)
    hn = (x * lax.rsqrt(ms + EPS) * g_ref[...]).astype(BF16)
    for o_ref, (lo, hi) in zip(out_refs, IN_SEGS):
        for c0 in range(lo, hi, 512):
            c1 = min(c0 + 512, hi)
            o_ref[:, c0 - lo:c1 - lo] = _dot(hn, w_ref[:, c0:c1]).astype(o_ref.dtype)


def _inproj(x2d, g, w_pad):
    n, d = x2d.shape
    tm = min(256, n)
    out_shape = [jax.ShapeDtypeStruct((n, hi - lo), dt) for (lo, hi), dt in zip(IN_SEGS, IN_DTYPES)]
    out_specs = [pl.BlockSpec((tm, hi - lo), lambda i: (i, 0)) for lo, hi in IN_SEGS]
    return pl.pallas_call(
        _inproj_kernel,
        grid=(n // tm,),
        in_specs=[
            pl.BlockSpec((tm, d), lambda i: (i, 0)),
            _const_spec((1, d)),
            pl.BlockSpec((d, IN_COLS_PAD), lambda i: (0, 0), pipeline_mode=pl.Buffered(1)),
        ],
        out_specs=out_specs,
        out_shape=out_shape,
        compiler_params=_cparams(("arbitrary",)),
        name="inproj",
    )(x2d, g, w_pad)


def _lru_kernel(ax_ref, ag_ref, cinit_ref, h0_ref, cw_ref, cb_ref, wr_ref, br_ref, wi_ref, bi_ref,
                lam_ref, oa_ref, hlast_ref, hist_sc, h_sc, *, sb, c):
    ci = pl.program_id(1)

    @pl.when(ci == 0)
    def _():
        hist_sc[...] = cinit_ref[...]
        h_sc[...] = h0_ref[...]

    w = LRU_WIDTH
    u = ax_ref[...]
    ext = jnp.concatenate([hist_sc[...], u], axis=1).reshape(sb * (SUBLANES + c), w)
    acc = cb_ref[...] + ext * cw_ref[3:4, :]
    for j in range(1, 4):
        acc = acc + pltpu.roll(ext, j, 0) * cw_ref[3 - j:4 - j, :]
    xc = acc.reshape(sb, SUBLANES + c, w)[:, SUBLANES:, :].reshape(sb * c, w)
    hist_sc[...] = u[:, c - SUBLANES:, :]

    xb = xc.astype(BF16)
    half = w // 2

    def gate(w_ref, b_ref):
        pre = jnp.concatenate([_dot(xb[:, :half], w_ref[0]), _dot(xb[:, half:], w_ref[1])], axis=1)
        return _sigmoid(pre + b_ref[...])

    r = gate(wr_ref, br_ref)
    i = gate(wi_ref, bi_ref)
    log_a = -LRU_C * r * _softplus(-lam_ref[...])
    a = jnp.exp(log_a)
    b = jnp.sqrt(1.0 - a * a) * (i * xc)

    t = _mod_pow2(lax.broadcasted_iota(jnp.int32, (sb * c, w), 0), c)
    s = 1
    while s < c:
        m = t >= s
        a_sh = pltpu.roll(a, s, 0)
        b_sh = pltpu.roll(b, s, 0)
        b = jnp.where(m, a * b_sh + b, b)
        a = jnp.where(m, a * a_sh, a)
        s *= 2

    h = a.reshape(sb, c, w) * h_sc[...] + b.reshape(sb, c, w)
    h_sc[...] = h[:, c - 1:c, :]
    ag = ag_ref[...].astype(F32)
    oa_ref[...] = (h * (ag * _sigmoid(ag))).astype(BF16)

    @pl.when(ci == pl.num_programs(1) - 1)
    def _():
        hlast_ref[...] = h_sc[...]


def _lru(ax, ag, cinit, h0, p, *, sb, c):
    n_seq, t, w = ax.shape
    kern = functools.partial(_lru_kernel, sb=sb, c=c)
    blk = lambda s: pl.BlockSpec(s, lambda i, j: (i, j, 0))
    return pl.pallas_call(
        kern,
        grid=(n_seq // sb, t // c),
        in_specs=[
            blk((sb, c, w)), blk((sb, c, w)),
            pl.BlockSpec((sb, SUBLANES, w), lambda i, j: (i, 0, 0)),
            pl.BlockSpec((sb, 1, w), lambda i, j: (i, 0, 0)),
            _const_spec((SUBLANES, w)), _const_spec((1, w)),
            _const_spec((2, w // 2, w // 2)), _const_spec((1, w)),
            _const_spec((2, w // 2, w // 2)), _const_spec((1, w)),
            _const_spec((1, w)),
        ],
        out_specs=[blk((sb, c, w)), pl.BlockSpec((sb, 1, w), lambda i, j: (i, 0, 0))],
        out_shape=[jax.ShapeDtypeStruct((n_seq, t, w), BF16), jax.ShapeDtypeStruct((n_seq, 1, w), F32)],
        scratch_shapes=[pltpu.VMEM((sb, SUBLANES, w), F32), pltpu.VMEM((sb, 1, w), F32)],
        compiler_params=_cparams(("arbitrary", "arbitrary")),
        name="rg_lru",
    )(ax, ag, cinit, h0, p["conv_w8"], p["conv_b"], p["lru_wr2"], p["lru_br"], p["lru_wi2"],
      p["lru_bi"], p["lru_lam"])


def _rwkv_kernel(bs_ref, bg_ref, pinit_ref, s0_ref, mu_ref, w0_ref, lora_ref, a0_ref, kkp_ref,
                 ka_ref, rk_ref, gng_ref, gnb_ref, bones_ref, ob_ref, sout_ref, hist_sc, s_sc,
                 *, sb, c):
    ci = pl.program_id(1)
    rows = sb * c
    hw = RW_WIDTH
    tb = RW_TB
    ntb = rows // tb
    spt = tb // c
    gt = RW_PAIRS * ntb
    gs = RW_PAIRS * sb

    @pl.when(ci == 0)
    def _():
        hist_sc[...] = pinit_ref[...]
        z = jnp.zeros((sb, RW_HD, RW_HD), F32)
        for p in range(RW_PAIRS):
            top = jnp.concatenate([s0_ref[:, 2 * p], z], axis=2)
            bot = jnp.concatenate([z, s0_ref[:, 2 * p + 1]], axis=2)
            s_sc[p] = jnp.concatenate([top, bot], axis=1)

    s3 = bs_ref[...]
    ext = jnp.concatenate([hist_sc[...], s3], axis=1).reshape(sb * (SUBLANES + c), SHIFT_COLS)
    sprev = pltpu.roll(ext, 1, 0).reshape(sb, SUBLANES + c, SHIFT_COLS)[:, SUBLANES:, :]
    sprev = sprev.reshape(rows, SHIFT_COLS)
    hist_sc[...] = s3[:, c - SUBLANES:, :]
    s = s3.reshape(rows, SHIFT_COLS)
    xs = s + (sprev - s) * mu_ref[...]
    r = xs[:, 0:hw]
    k = xs[:, hw:2 * hw]
    v = xs[:, 2 * hw:3 * hw]
    la = xs[:, 3 * hw:3 * hw + LANES]
    lane = lax.broadcasted_iota(jnp.int32, (rows, LANES), 1)
    lin = jnp.where(lane < RW_HD, jnp.tanh(la), la).astype(BF16)
    lo = _dot(lin, lora_ref[...])
    wlog = -_softplus(-(w0_ref[...] + lo[:, :hw])) - 0.5
    logw = -jnp.exp(wlog)
    a = _sigmoid(a0_ref[...] + lo[:, hw:])
    bones = bones_ref[...]
    kk = k * kkp_ref[...]
    kk = kk * lax.rsqrt(_dot_sel_r(kk * kk, bones) + 1e-12)
    kf = k * (1.0 + (a - 1.0) * ka_ref[...])
    b = kk * a

    ri = lax.broadcasted_iota(jnp.int32, (rows, rows), 0)
    cj = lax.broadcasted_iota(jnp.int32, (rows, rows), 1)
    same_seq = _div_pow2(ri, c) == _div_pow2(cj, c)
    ltri = jnp.where(same_seq & (cj <= ri), 1.0, 0.0).astype(BF16)
    lseq = jnp.where(same_seq, 1.0, 0.0).astype(BF16)
    log_g = _dot_sel_l3(ltri, logw)
    log_end = _dot_sel_l3(lseq, logw)
    g_inv = jnp.exp(-log_g)
    g_rem = jnp.exp(log_end - log_g)
    g_end = jnp.exp(log_end)

    low = lax.broadcasted_iota(jnp.int32, (1, 1, LANES), 2) < RW_HD

    def to_tb(x):
        return jnp.stack([x[tb * kb:tb * (kb + 1), LANES * p:LANES * (p + 1)]
                          for p in range(RW_PAIRS) for kb in range(ntb)], axis=0)

    def stack2(x3):
        return jnp.concatenate([jnp.where(low, x3, 0.0), jnp.where(low, 0.0, x3)], axis=1)

    def dup2(x3):
        return jnp.concatenate([x3, x3], axis=1)

    def to_ps(x3):
        if spt == 1:
            return x3
        return jnp.stack([jnp.concatenate([x3[g, c * q:c * (q + 1)], x3[g, tb + c * q:tb + c * (q + 1)]], axis=0)
                          for g in range(gt) for q in range(spt)], axis=0)

    def from_ps(y3):
        if spt == 1:
            return y3
        return jnp.stack([jnp.concatenate([y3[g * spt + q, :c] for q in range(spt)]
                                          + [y3[g * spt + q, c:] for q in range(spt)], axis=0)
                          for g in range(gt)], axis=0)

    xk = stack2(to_tb(kk * jnp.exp(log_g - logw)))
    xr = stack2(to_tb(r * jnp.exp(log_g)))
    vst = stack2(to_tb(v))
    kgs = stack2(to_tb(kf * g_rem))
    bgs = stack2(to_tb(b * g_rem))
    yb = dup2(to_tb(b * g_inv)).astype(BF16)
    yk = dup2(to_tb(kf * g_inv)).astype(BF16)
    xx = jnp.concatenate([xk, xr], axis=1).astype(BF16)
    ab = _bdot_nt(xx, yb)
    ak = _bdot_nt(xx, yk)

    n2 = 2 * tb
    i2 = lax.broadcasted_iota(jnp.int32, (1, n2, n2), 1)
    j2 = lax.broadcasted_iota(jnp.int32, (1, n2, n2), 2)
    same_blk = _div_pow2(i2, c) == _div_pow2(j2, c)
    m_strict = same_blk & (j2 < i2)
    m_incl = same_blk & (j2 <= i2)
    eye = jnp.where(i2 == j2, 1.0, 0.0)
    a_bk = jnp.where(m_strict, ab[:, :n2], 0.0)
    a_br = jnp.where(m_incl, ab[:, n2:], 0.0)
    a_kk = jnp.where(m_strict, ak[:, :n2], 0.0)
    a_kr = jnp.where(m_incl, ak[:, n2:], 0.0)

    pw = -a_bk
    tinv = eye + pw
    for _ in range(int(math.log2(c)) - 1):
        pw = _bdot16(pw, pw)
        tinv = tinv + _bdot16(tinv, pw)

    st = s_sc[...].reshape(gs, LANES, LANES)
    stb = st.astype(BF16)
    if spt == 1:
        res = _bdot_nt(xx, stb)
        xks, xrs = res[:, :n2], res[:, n2:]
    else:
        xks = from_ps(_bdot_nt(to_ps(xk).astype(BF16), stb))
        xrs = from_ps(_bdot_nt(to_ps(xr).astype(BF16), stb))
    vb = vst.astype(BF16)
    uu = _bdot16(tinv, xks + _bdot(a_kk.astype(BF16), vb))
    ost = xrs + _bdot(a_kr.astype(BF16), vb) - _bdot(a_br.astype(BF16), uu.astype(BF16))
    o3 = ost[:, :tb] + ost[:, tb:]

    lhs = jnp.concatenate([to_ps(vst), -to_ps(uu)], axis=1).astype(BF16)
    rhs = jnp.concatenate([to_ps(kgs), to_ps(bgs)], axis=1).astype(BF16)
    ds_ = _bdot_tn(lhs, rhs)
    gend = jnp.stack([g_end[c * q:c * q + 1, LANES * p:LANES * (p + 1)]
                      for p in range(RW_PAIRS) for q in range(sb)], axis=0)
    s_sc[...] = (st * gend + ds_).reshape(RW_PAIRS, sb, LANES, LANES)

    o = jnp.concatenate([jnp.concatenate([o3[p * ntb + kb] for kb in range(ntb)], axis=0)
                         for p in range(RW_PAIRS)], axis=1)
    inv_hd = 1.0 / RW_HD
    mean = _dot_sel_r(o, bones) * inv_hd
    d = o - mean
    var = _dot_sel_r(d * d, bones) * inv_hd
    on = d * lax.rsqrt(var + GN_EPS) * gng_ref[...] + gnb_ref[...]
    bonus = _dot_sel_r(r * kf * rk_ref[...], bones) * v
    g = bg_ref[...].astype(F32).reshape(rows, hw)
    ob_ref[...] = ((on + bonus) * (g * _sigmoid(g))).astype(BF16).reshape(sb, c, hw)

    @pl.when(ci == pl.num_programs(1) - 1)
    def _():
        for p in range(RW_PAIRS):
            sp = s_sc[p]
            sout_ref[:, 2 * p] = sp[:, :RW_HD, :RW_HD]
            sout_ref[:, 2 * p + 1] = sp[:, RW_HD:, RW_HD:]


def _rwkv(bs, bg, pinit, s0, p, *, sb, c):
    n_seq, t, _ = bs.shape
    hw = RW_WIDTH
    assert (sb * c) % RW_TB == 0 and RW_TB % c == 0
    kern = functools.partial(_rwkv_kernel, sb=sb, c=c)
    blk = lambda s: pl.BlockSpec(s, lambda i, j: (i, j, 0))
    st_spec = pl.BlockSpec((sb, 2 * RW_PAIRS, RW_HD, RW_HD), lambda i, j: (i, 0, 0, 0))
    return pl.pallas_call(
        kern,
        grid=(n_seq // sb, t // c),
        in_specs=[
            blk((sb, c, SHIFT_COLS)), blk((sb, c, hw)),
            pl.BlockSpec((sb, SUBLANES, SHIFT_COLS), lambda i, j: (i, 0, 0)),
            st_spec,
            _const_spec((1, SHIFT_COLS)), _const_spec((1, hw)), _const_spec((LANES, 2 * hw)),
            _const_spec((1, hw)), _const_spec((1, hw)), _const_spec((1, hw)), _const_spec((1, hw)),
            _const_spec((1, hw)), _const_spec((1, hw)), _const_spec((hw, hw)),
        ],
        out_specs=[blk((sb, c, hw)), st_spec],
        out_shape=[jax.ShapeDtypeStruct((n_seq, t, hw), BF16),
                   jax.ShapeDtypeStruct((n_seq, 2 * RW_PAIRS, RW_HD, RW_HD), F32)],
        scratch_shapes=[pltpu.VMEM((sb, SUBLANES, SHIFT_COLS), F32),
                        pltpu.VMEM((RW_PAIRS, sb, LANES, LANES), F32)],
        compiler_params=_cparams(("arbitrary", "arbitrary")),
        name="rwkv7",
    )(bs, bg, pinit, s0, p["rw_mu"], p["rw_w0"], p["rw_lora"], p["rw_a0"], p["rw_kk"], p["rw_ka"],
      p["rw_rk"], p["rw_gn_g"], p["rw_gn_b"], p["bones"])


def _rope128(x, cos, sin_a, sin_b):
    return x * cos + pltpu.roll(x, LANES - ROPE_DIM // 2, 1) * sin_a + pltpu.roll(x, ROPE_DIM // 2, 1) * sin_b


def _mla_prep_kernel(cq_ref, ckv_ref, pe_ref, cos_ref, sa_ref, sb_ref, gqa_ref, wq_ref, gq_ref,
                     gkva_ref, wk_ref, gk_ref, wvt_ref, ckvn_ref, q_ref, *kv_refs, sb, tt):
    rows = sb * tt
    cq = cq_ref[...].reshape(rows, Q_RANK)
    cqn = (cq * lax.rsqrt(jnp.mean(cq * cq, axis=-1, keepdims=True) + EPS) * gqa_ref[...]).astype(BF16)
    q = _dot(cqn, wq_ref[...])
    ckv = ckv_ref[...].reshape(rows, KV_RANK)
    ckvn = ckv * lax.rsqrt(jnp.mean(ckv * ckv, axis=-1, keepdims=True) + EPS) * gkva_ref[...]
    ckvn_ref[...] = ckvn.reshape(sb, tt, KV_RANK)

    def tile(ref):
        return jnp.broadcast_to(ref[...][None], (sb, tt, LANES)).reshape(rows, LANES)

    cos, sa, sb_ = tile(cos_ref), tile(sa_ref), tile(sb_ref)
    inv_d = 1.0 / QK_DIM
    for h in range(MLA_HEADS):
        qh = q[:, h * LANES:(h + 1) * LANES]
        qn = qh * lax.rsqrt(jnp.sum(qh * qh, axis=-1, keepdims=True) * inv_d + EPS) * gq_ref[...]
        q_ref[:, h] = (_rope128(qn, cos, sa, sb_) * SCALE).astype(BF16).reshape(sb, tt, LANES)
    if not kv_refs:
        return
    k_ref, vt_ref = kv_refs
    cb = ckvn.astype(BF16)
    kvk = _dot(cb, wk_ref[...])
    vt = _dot_nt(wvt_ref[...], cb)
    pe = pe_ref[...].reshape(rows, LANES)
    extra = lax.broadcasted_iota(jnp.int32, (VT_ROWS - V_DIM, rows), 0)
    ones_rows = jnp.where(extra == 0, 1.0, 0.0)
    for h in range(MLA_HEADS):
        kh = kvk[:, h * LANES:(h + 1) * LANES] + pe
        kn = kh * lax.rsqrt(jnp.sum(kh * kh, axis=-1, keepdims=True) * inv_d + EPS) * gk_ref[...]
        k_ref[:, h] = _rope128(kn, cos, sa, sb_).astype(BF16).reshape(sb, tt, LANES)
        vt_ref[0, h] = jnp.concatenate([vt[h * V_DIM:(h + 1) * V_DIM], ones_rows], axis=0).astype(BF16)


def _mla_prep(cq, ckv, pe_a, tabs, p, *, sb, tt, with_kv):
    n_seq, t, _ = cq.shape
    assert sb == 1 or not with_kv
    kern = functools.partial(_mla_prep_kernel, sb=sb, tt=tt)
    blk = lambda w: pl.BlockSpec((sb, tt, w), lambda i, j: (i, j, 0))
    tab = pl.BlockSpec((tt, LANES), lambda i, j: (j, 0))
    hd = pl.BlockSpec((sb, MLA_HEADS, tt, LANES), lambda i, j: (i, 0, j, 0))
    hshape = jax.ShapeDtypeStruct((n_seq, MLA_HEADS, t, LANES), BF16)
    out_specs = [blk(KV_RANK), hd]
    out_shape = [jax.ShapeDtypeStruct((n_seq, t, KV_RANK), F32), hshape]
    if with_kv:
        out_specs += [hd, pl.BlockSpec((sb, MLA_HEADS, VT_ROWS, tt), lambda i, j: (i, 0, 0, j))]
        out_shape += [hshape, jax.ShapeDtypeStruct((n_seq, MLA_HEADS, VT_ROWS, t), BF16)]
    return pl.pallas_call(
        kern,
        grid=(n_seq // sb, t // tt),
        in_specs=[
            blk(Q_RANK), blk(KV_RANK), blk(LANES), tab, tab, tab,
            _const_spec((1, Q_RANK)), _const_spec((Q_RANK, MLA_HEADS * LANES)), _const_spec((1, LANES)),
            _const_spec((1, KV_RANK)), _const_spec((KV_RANK, MLA_HEADS * LANES)), _const_spec((1, LANES)),
            _const_spec((MLA_WIDTH, KV_RANK)),
        ],
        out_specs=out_specs,
        out_shape=out_shape,
        compiler_params=_cparams(("arbitrary", "arbitrary")),
        name="mla_prep",
    )(cq, ckv, pe_a, tabs[0], tabs[1], tabs[2], p["g_qa"], p["wq"], p["gq128"], p["g_kva"],
      p["wk"], p["gk128"], p["wvt"])


def _flash_kernel(q_ref, k_ref, vt_ref, o_ref, *, tq, tk):
    qi = pl.program_id(2)
    nsub = tq // tk
    qs = [q_ref[0, hh] for hh in range(2)]

    def tile(hh, j, carry, q0, mask):
        m, acc = carry
        kj = k_ref[0, hh, pl.ds(j * tk, tk), :]
        vj = vt_ref[0, hh, :, pl.ds(j * tk, tk)].astype(F32)
        s = _dot_nt(kj, qs[hh][q0:])
        if mask is not None:
            s = jnp.where(mask, s, NEG)
        m_old = m[:, q0:]
        m_new = jnp.maximum(m_old, jnp.max(s, axis=0, keepdims=True))
        pr = jnp.exp(s - m_new)
        acc_new = acc[:, q0:] * jnp.exp(m_old - m_new) + _dot(vj, pr)
        if q0:
            m_new = jnp.concatenate([m[:, :q0], m_new], axis=1)
            acc_new = jnp.concatenate([acc[:, :q0], acc_new], axis=1)
        return m_new, acc_new

    def step(j, carry):
        return tuple(tile(hh, j, carry[hh], 0, None) for hh in range(2))

    init1 = (jnp.full((1, tq), NEG, F32), jnp.zeros((VT_ROWS, tq), F32))
    carry = lax.fori_loop(0, qi * nsub, step, (init1, init1))
    for d in range(nsub):
        q0 = d * tk
        krow = lax.broadcasted_iota(jnp.int32, (tk, tq - q0), 0)
        qcol = lax.broadcasted_iota(jnp.int32, (tk, tq - q0), 1)
        carry = tuple(tile(hh, qi * nsub + d, carry[hh], q0, krow <= qcol) for hh in range(2))
    ot = jnp.concatenate([carry[hh][1][:V_DIM] / carry[hh][1][V_DIM:V_DIM + 1] for hh in range(2)],
                         axis=0)
    o_ref[0] = ot.T.astype(BF16)


def _flash(q, k, vt, *, tq, tk):
    b, h, t, _ = q.shape
    kern = functools.partial(_flash_kernel, tq=tq, tk=tk)
    return pl.pallas_call(
        kern,
        grid=(b, h // 2, t // tq),
        in_specs=[
            pl.BlockSpec((1, 2, tq, LANES), lambda bi, pi, qi: (bi, pi, qi, 0)),
            pl.BlockSpec((1, 2, t, LANES), lambda bi, pi, qi: (bi, pi, 0, 0)),
            pl.BlockSpec((1, 2, VT_ROWS, t), lambda bi, pi, qi: (bi, pi, 0, 0)),
        ],
        out_specs=pl.BlockSpec((1, tq, LANES), lambda bi, pi, qi: (bi, qi, pi)),
        out_shape=jax.ShapeDtypeStruct((b, t, (h // 2) * LANES), BF16),
        compiler_params=_cparams(("arbitrary", "arbitrary", "arbitrary")),
        name="mla_flash",
    )(q, k, vt)


def _decode_kernel(pt_ref, q_ref, cnew_ref, penew_ref, ckv_hbm, kpe_hbm, wukt_ref, gkn_ref, sel_ref,
                   wukr_ref, gkr_ref, gkrt_ref, ct_ref, st_ref, tn_ref, wuv_ref, o_ref,
                   cbuf, pbuf, s_sc, cb_sc, sem, *, layer, pc, n_chunks, ts):
    b = pl.program_id(0)
    nb = pl.num_programs(0)
    page = ckv_hbm.shape[2]
    hq = MLA_HEADS * ts
    half = ROPE_DIM // 2

    def copies(bb, ch, slot):
        out = []
        for i in range(pc):
            phys = pt_ref[bb, ch * pc + i]
            out.append(pltpu.make_async_copy(
                ckv_hbm.at[layer, phys], cbuf.at[slot, pl.ds(i * page, page), :], sem.at[0, slot]))
            out.append(pltpu.make_async_copy(
                kpe_hbm.at[layer, phys], pbuf.at[slot, :, pl.ds(i * page, page)], sem.at[1, slot]))
        return out

    @pl.when(b == 0)
    def _():
        for cp in copies(0, 0, 0):
            cp.start()

    qf = q_ref[0].astype(F32).reshape(hq, LANES)
    qg = qf * gkn_ref[...]
    qlat = jnp.concatenate(
        [_dot(qg[h * ts:(h + 1) * ts].astype(BF16), wukt_ref[h]) for h in range(MLA_HEADS)],
        axis=0).astype(BF16)
    qrope = _dot(qf.astype(BF16), sel_ref[...])
    qrope32 = qrope[:, :ROPE_DIM].astype(BF16)
    qrope128 = qrope.astype(BF16)
    inv_d = 1.0 / QK_DIM
    n_nope = MLA_HEADS * NOPE_DIM
    lhs_all = jnp.concatenate([wukr_ref[...], qlat], axis=0)

    def latent_scores(cb):
        res = _dot_nt(lhs_all, cb)
        rows = []
        for h in range(MLA_HEADS):
            kt = res[h * NOPE_DIM:(h + 1) * NOPE_DIM]
            ssh = jnp.sum(kt * kt, axis=0, keepdims=True)
            rows.append(jnp.broadcast_to(ssh, (ts, ssh.shape[1])))
        return res[n_nope:], jnp.concatenate(rows, axis=0)

    def update(carry, s, cb):
        m, l, acc = carry
        m_new = jnp.maximum(m, jnp.max(s, axis=-1, keepdims=True))
        pr = jnp.exp(s - m_new)
        corr = jnp.exp(m - m_new)
        l = l * corr + jnp.sum(pr, axis=-1, keepdims=True)
        acc = acc * corr + _dot(pr.astype(BF16), cb)
        return m_new, l, acc

    def dma_step(ch):
        slot = (b * n_chunks + ch) % 2
        for cp in copies(b, ch, slot):
            cp.wait()

        @pl.when(ch + 1 < n_chunks)
        def _():
            for cp in copies(b, ch + 1, 1 - slot):
                cp.start()

        @pl.when((ch + 1 == n_chunks) & (b + 1 < nb))
        def _():
            for cp in copies(b + 1, 0, 1 - slot):
                cp.start()
        return slot

    def stage_a(ch, slot, par):
        krs, pss = [], []
        for i in range(pc):
            pg = ch * pc + i
            x = pbuf[slot, :, i * page:(i + 1) * page]
            xg = x * gkrt_ref[...]
            sw = jnp.concatenate([xg[half:], xg[:half]], axis=0)
            krs.append((xg * ct_ref[pg] + sw * st_ref[pg]).astype(BF16))
            pss.append(jnp.sum(x * x, axis=0, keepdims=True))
        cb = cbuf[slot].astype(BF16)
        kr = jnp.concatenate(krs, axis=1)
        pesq = jnp.concatenate(pss, axis=1)
        s_lat, ssn = latent_scores(cb)
        s = s_lat + _dot(qrope32, kr)
        s_sc[par] = s * lax.rsqrt((ssn + pesq) * inv_d + EPS)
        cb_sc[par] = cb

    def step(ch, par, carry):
        slot = dma_step(ch)
        stage_a(ch, slot, par)
        return update(carry, s_sc[1 - par], cb_sc[1 - par])

    init = (jnp.full((hq, 1), NEG, F32), jnp.zeros((hq, 1), F32), jnp.zeros((hq, LANES), F32))
    stage_a(0, dma_step(0), 0)
    n_steps = n_chunks - 1

    def pair(j, carry):
        carry = step(2 * j + 1, 1, carry)
        return step(2 * j + 2, 0, carry)

    carry = lax.fori_loop(0, n_steps // 2, pair, init)
    if n_steps % 2:
        carry = step(n_chunks - 1, 1, carry)
    last = (n_chunks - 1) % 2
    carry = update(carry, s_sc[last], cb_sc[last])

    pad = jnp.zeros((NEW_ROWS - ts, LANES), F32)
    cn = jnp.concatenate([cnew_ref[0], pad], axis=0)
    pn = jnp.concatenate([penew_ref[0], pad], axis=0)
    krn = _rope128(pn * gkr_ref[...], tn_ref[0], tn_ref[1], tn_ref[2]).astype(BF16)
    cnb = cn.astype(BF16)
    ones_r = jnp.ones((hq, LANES), BF16)
    s_lat, ssn = latent_scores(cnb)
    ssq = ssn + _dot_nt(ones_r, (pn * pn).astype(BF16))
    s = (s_lat + _dot_nt(qrope128, krn)) * lax.rsqrt(ssq * inv_d + EPS)
    qidx = _mod_pow2(lax.broadcasted_iota(jnp.int32, (hq, NEW_ROWS), 0), ts)
    tidx = lax.broadcasted_iota(jnp.int32, (hq, NEW_ROWS), 1)
    s = jnp.where(tidx <= qidx, s, NEG)
    _, l, acc = update(carry, s, cnb)
    al = acc / l
    out = _dot(al[0:ts].astype(BF16), wuv_ref[0])
    for h in range(1, MLA_HEADS):
        out = out + _dot(al[h * ts:(h + 1) * ts].astype(BF16), wuv_ref[h])
    o_ref[0] = out


def _decode(page_table, q, cnew, penew, cache_ckv, cache_kpe_t, tabs_c, tabs_s, tabs_n, p, *, layer, pc):
    db, h, ts, _ = q.shape
    n_pages = page_table.shape[1]
    page = cache_ckv.shape[2]
    assert n_pages % pc == 0
    n_chunks = n_pages // pc
    tok = pc * page
    kern = functools.partial(_decode_kernel, layer=layer, pc=pc, n_chunks=n_chunks, ts=ts)
    cs = lambda shape: pl.BlockSpec(shape, lambda b, pt: (0,) * len(shape))
    cs1 = lambda shape: pl.BlockSpec(shape, lambda b, pt: (0,) * len(shape), pipeline_mode=pl.Buffered(1))
    grid_spec = pltpu.PrefetchScalarGridSpec(
        num_scalar_prefetch=1,
        grid=(db,),
        in_specs=[
            pl.BlockSpec((1, h, ts, LANES), lambda b, pt: (b, 0, 0, 0)),
            pl.BlockSpec((1, ts, LANES), lambda b, pt: (b, 0, 0)),
            pl.BlockSpec((1, ts, LANES), lambda b, pt: (b, 0, 0)),
            pl.BlockSpec(memory_space=pl.ANY),
            pl.BlockSpec(memory_space=pl.ANY),
            cs((h, LANES, LANES)), cs((1, LANES)), cs((LANES, LANES)),
            cs((h * NOPE_DIM, KV_RANK)), cs((1, LANES)), cs((ROPE_DIM, LANES)),
            cs1((n_pages, ROPE_DIM, page)), cs1((n_pages, ROPE_DIM, page)), cs((3, NEW_ROWS, LANES)),
            cs((h, KV_RANK, MLA_WIDTH)),
        ],
        out_specs=pl.BlockSpec((1, ts, MLA_WIDTH), lambda b, pt: (b, 0, 0)),
        scratch_shapes=[
            pltpu.VMEM((2, tok, KV_RANK), F32),
            pltpu.VMEM((2, ROPE_DIM, tok), F32),
            pltpu.VMEM((2, h * ts, tok), F32),
            pltpu.VMEM((2, tok, KV_RANK), BF16),
            pltpu.SemaphoreType.DMA((2, 2)),
        ],
    )
    return pl.pallas_call(
        kern,
        grid_spec=grid_spec,
        out_shape=jax.ShapeDtypeStruct((db, ts, MLA_WIDTH), F32),
        compiler_params=_cparams(("arbitrary",)),
        name="mla_decode",
    )(page_table, q, cnew, penew, cache_ckv, cache_kpe_t, p["wukt"], p["gkn128"], p["sel_rope"],
      p["wukr"], p["gkr128"], p["gkrt"], tabs_c, tabs_s, tabs_n, p["wuv"])


def _merge_kernel(x_ref, gate_ref, oa_ref, ob_ref, oc_ref, cg_ref, woa_ref, wob_ref, woc_ref,
                  wout_ref, y_ref):
    d = x_ref.shape[1]
    cg = cg_ref[...].astype(F32)
    oc = (oc_ref[...].astype(F32) * (cg * _sigmoid(cg))).astype(BF16)
    g = _sigmoid(gate_ref[...].astype(F32))
    merged = (g[:, 0:d] * _dot(oa_ref[...], woa_ref[...])
              + g[:, d:2 * d] * _dot(ob_ref[...], wob_ref[...])
              + g[:, 2 * d:3 * d] * _dot(oc, woc_ref[...]))
    y_ref[...] = x_ref[...] + _dot(merged.astype(BF16), wout_ref[...])


def _merge(x2d, gate, oa, ob, oc, cg, p):
    n, d = x2d.shape
    tm = min(512, n)
    row = lambda w: pl.BlockSpec((tm, w), lambda i: (i, 0))
    return pl.pallas_call(
        _merge_kernel,
        grid=(n // tm,),
        in_specs=[row(d), row(3 * d), row(LRU_WIDTH), row(RW_WIDTH), row(MLA_WIDTH), row(MLA_WIDTH),
                  _const_spec((LRU_WIDTH, d)), _const_spec((RW_WIDTH, d)), _const_spec((MLA_WIDTH, d)),
                  _const_spec((d, d))],
        out_specs=row(d),
        out_shape=jax.ShapeDtypeStruct((n, d), F32),
        compiler_params=_cparams(("arbitrary",)),
        name="merge",
    )(x2d, gate, oa, ob, oc, cg, p["w_oa"], p["w_ob"], p["w_oc"], p["w_out"])


def _rope_angles(pos):
    inv = 1.0 / (ROPE_THETA ** (jnp.arange(0, ROPE_DIM, 2, dtype=F32) / ROPE_DIM))
    ang = pos.astype(F32)[:, None] * inv[None, :]
    return jnp.cos(ang), jnp.sin(ang)


def _rope_pattern(pos, lane0, rows=None):
    half = ROPE_DIM // 2
    cos, sin = _rope_angles(pos)
    n = pos.shape[0]
    ones_before = jnp.ones((n, lane0), F32) if lane0 else jnp.zeros((n, 0), F32)
    z = lambda w: jnp.zeros((n, w), F32)
    rest = LANES - lane0 - ROPE_DIM
    cos_t = jnp.concatenate([ones_before, cos, cos, z(rest)], axis=1)
    sin_a = jnp.concatenate([z(lane0), -sin, z(half), z(rest)], axis=1)
    sin_b = jnp.concatenate([z(lane0), z(half), sin, z(rest)], axis=1)
    tabs = jnp.stack([cos_t, sin_a, sin_b], axis=0)
    if rows is not None and rows > n:
        tabs = jnp.pad(tabs, ((0, 0), (0, rows - n), (0, 0)))
    return tabs


def _rope_pages(n_pages, page):
    cos, sin = _rope_angles(jnp.arange(n_pages * page))
    ct = jnp.concatenate([cos, cos], axis=1)
    st = jnp.concatenate([-sin, sin], axis=1)
    lay = lambda x: jnp.transpose(x.reshape(n_pages, page, ROPE_DIM), (0, 2, 1))
    return lay(ct), lay(st)


def _block_diag2(w):
    nb, bs, _ = w.shape
    per = nb // 2
    out = jnp.zeros((2, per * bs, per * bs), w.dtype)
    for i in range(nb):
        hf, j = divmod(i, per)
        out = out.at[hf, j * bs:(j + 1) * bs, j * bs:(j + 1) * bs].set(w[i])
    return out


def _layer_params(l, ts, a):
    d = a["w_in"].shape[1]
    w = a["w_in"][l]
    src_pe = SEG_CKV[1]
    z = lambda n: jnp.zeros((d, n), w.dtype)
    w_pad = jnp.concatenate([
        w[:, :src_pe],
        z(64), w[:, src_pe:src_pe + ROPE_DIM], z(32),
        w[:, src_pe:src_pe + ROPE_DIM], z(96),
        w[:, src_pe + ROPE_DIM:],
    ], axis=1).astype(BF16)
    assert w_pad.shape[1] == IN_COLS_PAD
    row = lambda v: v.reshape(1, -1)
    hd = RW_HD
    bones = jnp.kron(jnp.eye(RW_WIDTH // hd, dtype=F32), jnp.ones((hd, hd), F32)).astype(BF16)
    zl = jnp.zeros((LORA, RW_WIDTH), F32)
    lora = jnp.concatenate([jnp.concatenate([a["rw_w2"][l], zl], axis=1),
                            jnp.concatenate([zl, a["rw_a2"][l]], axis=1)], axis=0).astype(BF16)
    w_uq = a["mla_w_uq"][l]
    w_ukv = a["mla_w_ukv"][l]
    w_uk, w_uv = w_ukv[..., :NOPE_DIM], w_ukv[..., NOPE_DIM:]
    g_kn = a["mla_g_kn"][l]
    pad_h = lambda x: jnp.pad(x, ((0, 0), (0, 0), (0, LANES - x.shape[-1]))).reshape(x.shape[0], -1)
    wukt = jnp.pad(jnp.transpose(w_uk, (1, 2, 0)), ((0, 0), (0, LANES - NOPE_DIM), (0, 0)))
    wukr = jnp.transpose(w_uk.reshape(KV_RANK, MLA_HEADS * NOPE_DIM))
    wuv = jnp.zeros((MLA_HEADS, KV_RANK, MLA_WIDTH), F32)
    for h in range(MLA_HEADS):
        wuv = wuv.at[h, :, h * V_DIM:(h + 1) * V_DIM].set(w_uv[:, h, :])
    sel = jnp.zeros((LANES, LANES), F32).at[
        jnp.arange(NOPE_DIM, QK_DIM), jnp.arange(ROPE_DIM)].set(1.0)
    return {
        "norm_g": row(a["norm_g"][l]), "w_in": w_pad,
        "conv_w8": jnp.pad(a["conv_w"][l], ((0, SUBLANES - a["conv_w"].shape[1]), (0, 0))),
        "conv_b": row(a["conv_b"][l]),
        "lru_wr2": _block_diag2(a["lru_wr"][l]).astype(BF16), "lru_br": row(a["lru_br"][l]),
        "lru_wi2": _block_diag2(a["lru_wi"][l]).astype(BF16), "lru_bi": row(a["lru_bi"][l]),
        "lru_lam": row(a["lru_lam"][l]),
        "rw_mu": row(a["rw_mu"][l]), "rw_w0": row(a["rw_w0"][l]), "rw_lora": lora,
        "rw_a0": row(a["rw_a0"][l]), "rw_kk": row(a["rw_kk"][l]), "rw_ka": row(a["rw_ka"][l]),
        "rw_rk": row(a["rw_rk"][l]), "rw_gn_g": row(a["rw_gn_g"][l]), "rw_gn_b": row(a["rw_gn_b"][l]),
        "bones": bones,
        "g_qa": row(a["mla_g_qa"][l]), "wq": pad_h(w_uq).astype(BF16),
        "gq128": row(jnp.pad(a["mla_g_qn"][l], (0, LANES - QK_DIM))),
        "g_kva": row(a["mla_g_kva"][l]), "wk": pad_h(w_uk).astype(BF16),
        "gk128": row(jnp.pad(g_kn, (0, LANES - QK_DIM))),
        "wvt": jnp.transpose(w_uv.reshape(KV_RANK, MLA_WIDTH)).astype(BF16),
        "wukt": wukt.astype(BF16),
        "gkn128": row(jnp.pad(g_kn[:NOPE_DIM], (0, LANES - NOPE_DIM))),
        "sel_rope": sel.astype(BF16), "wukr": wukr.astype(BF16),
        "gkr128": row(jnp.pad(g_kn[NOPE_DIM:], (0, LANES - ROPE_DIM))),
        "gkrt": jnp.broadcast_to(g_kn[NOPE_DIM:, None], (ROPE_DIM, LANES)),
        "wuv": wuv.astype(BF16),
        "w_oa": a["w_oa"][l].astype(BF16), "w_ob": a["w_ob"][l].astype(BF16),
        "w_oc": a["w_oc"][l].astype(BF16), "w_out": a["w_out"][l].astype(BF16),
    }


def _hist_rows(x):
    return jnp.pad(x, ((0, 0), (SUBLANES - x.shape[1], 0), (0, 0)))


def _mixer_layer(x, p, lru_h0, conv_buf, rw_s0, rw_prev, tabs, attend, *, lru_blk, rw_blk, prep_blk,
                 with_kv):
    n_seq, t, d = x.shape
    n = n_seq * t
    segs = _inproj(x.reshape(n, d), p["norm_g"], p["w_in"])
    ax, ag, bs, bg, cq, ckv_raw, pe_a, pe_b, cg, gate = segs
    r3 = lambda v: v.reshape(n_seq, t, v.shape[-1])
    oa, lru_h = _lru(r3(ax), r3(ag), _hist_rows(conv_buf), lru_h0[:, None, :], p, sb=lru_blk[0],
                     c=lru_blk[1])
    lru_h = lru_h[:, 0, :]
    ob, rw_s = _rwkv(r3(bs), r3(bg), _hist_rows(rw_prev[:, None, :]), rw_s0, p, sb=rw_blk[0], c=rw_blk[1])
    prep = _mla_prep(r3(cq), r3(ckv_raw), r3(pe_a), tabs, p, sb=prep_blk[0], tt=prep_blk[1],
                     with_kv=with_kv)
    ckv = prep[0]
    oc = attend(prep, r3(pe_b))
    y = _merge(x.reshape(n, d), gate, oa.reshape(n, -1), ob.reshape(n, -1), oc.reshape(n, -1), cg, p)
    conv_new = r3(ax)[:, t - conv_buf.shape[1]:, :]
    shift_new = r3(bs)[:, t - 1, :]
    kpe = r3(pe_b)[:, :, :ROPE_DIM]
    return y.reshape(n_seq, t, d), (lru_h, conv_new, rw_s, shift_new, ckv, kpe)


def kernel(x_prompt, x_sample, state_lru_h, state_lru_conv, state_rwkv_S, state_rwkv_shift, cache_ckv, cache_kpe, page_table, norm_g, w_in, conv_w, conv_b, lru_wr, lru_br, lru_wi, lru_bi, lru_lam, rw_mu, rw_w0, rw_w2, rw_a0, rw_a2, rw_kk, rw_ka, rw_rk, rw_gn_g, rw_gn_b, mla_g_qa, mla_g_kva, mla_w_uq, mla_w_ukv, mla_g_qn, mla_g_kn, w_oa, w_ob, w_oc, w_out):
    a = dict(norm_g=norm_g, w_in=w_in, conv_w=conv_w, conv_b=conv_b, lru_wr=lru_wr, lru_br=lru_br,
             lru_wi=lru_wi, lru_bi=lru_bi, lru_lam=lru_lam, rw_mu=rw_mu, rw_w0=rw_w0, rw_w2=rw_w2,
             rw_a0=rw_a0, rw_a2=rw_a2, rw_kk=rw_kk, rw_ka=rw_ka, rw_rk=rw_rk, rw_gn_g=rw_gn_g,
             rw_gn_b=rw_gn_b, mla_g_qa=mla_g_qa, mla_g_kva=mla_g_kva, mla_w_uq=mla_w_uq,
             mla_w_ukv=mla_w_ukv, mla_g_qn=mla_g_qn, mla_g_kn=mla_g_kn, w_oa=w_oa, w_ob=w_ob,
             w_oc=w_oc, w_out=w_out)
    bsz, t, _ = x_prompt.shape
    db, ts, _ = x_sample.shape
    depth = w_in.shape[0]
    n_pages = page_table.shape[1]
    page = cache_ckv.shape[2]
    past = n_pages * page
    dt = x_prompt.dtype
    assert ts == SUBLANES and t % RW_TB == 0 and db % SUBLANES == 0

    tabs_p = _rope_pattern(jnp.arange(t), NOPE_DIM)
    tabs_s = _rope_pattern(past + jnp.arange(ts), NOPE_DIM)
    dec_c, dec_s = _rope_pages(n_pages, page)
    dec_n = _rope_pattern(past + jnp.arange(ts), 0, rows=NEW_ROWS)
    cache_kpe_t = jnp.transpose(cache_kpe, (0, 1, 3, 2))

    tq = min(FLASH_TQ, t)
    tk = min(FLASH_TK, t)
    pc = min(DECODE_PAGES, n_pages)
    lru_c = min(256, t)
    prep_tt = min(512, t)
    sb_s = min(32, db)

    hp, hs = x_prompt, x_sample
    st_p, st_s = [], []
    for l in range(depth):
        p = _layer_params(l, ts, a)

        def attend_prompt(prep, pe_b):
            _, q, k, vt = prep
            return _flash(q, k, vt, tq=tq, tk=tk)

        def attend_sample(prep, pe_b, l=l, p=p):
            ckv, q = prep
            return _decode(page_table, q, ckv, pe_b, cache_ckv, cache_kpe_t, dec_c, dec_s, dec_n, p,
                           layer=l, pc=pc)

        hp, sp = _mixer_layer(
            hp, p, jnp.zeros((bsz, LRU_WIDTH), dt), jnp.zeros((bsz, 3, LRU_WIDTH), dt),
            jnp.zeros((bsz, 2 * RW_PAIRS, RW_HD, RW_HD), dt),
            jnp.zeros((bsz, SHIFT_COLS), dt), tabs_p, attend_prompt, lru_blk=(bsz, lru_c),
            rw_blk=(bsz, RW_TB), prep_blk=(1, prep_tt), with_kv=True)
        hs, ss = _mixer_layer(
            hs, p, state_lru_h[l], state_lru_conv[l], state_rwkv_S[l], state_rwkv_shift[l],
            tabs_s, attend_sample, lru_blk=(sb_s, ts), rw_blk=(RW_TB // ts, ts),
            prep_blk=(sb_s, ts), with_kv=False)
        st_p.append(sp)
        st_s.append(ss)

    stack = lambda states, i: jnp.stack([s[i] for s in states], axis=0)
    return (hp, hs) + tuple(stack(st_p, i) for i in range(6)) + tuple(stack(st_s, i) for i in range(6))
```
